```python
import math
import jax, jax.numpy as jnp
from jax import lax
import numpy as np

D_MODEL = 2048
BATCH = 4
SEQ = 4096
DEPTH = 2

N_MIXERS = 2
RMS_EPS = 1e-6
SSD_EXPAND = 2
SSD_D_INNER = SSD_EXPAND * D_MODEL
SSD_HEAD_DIM = 64
SSD_HEADS = SSD_D_INNER // SSD_HEAD_DIM
SSD_GROUPS = 8
SSD_HEADS_PER_GROUP = SSD_HEADS // SSD_GROUPS
SSD_STATE = 128
SSD_CONV = 4
SSD_CHUNK = 256
SSD_CONV_DIM = SSD_D_INNER + 2 * SSD_GROUPS * SSD_STATE
SSD_IN_DIM = 2 * SSD_D_INNER + 2 * SSD_GROUPS * SSD_STATE + SSD_HEADS
S5_WIDTH = D_MODEL
S5_GROUP_CH = 16
S5_GROUPS = S5_WIDTH // S5_GROUP_CH
S5_STATE = 64
N_EXPERTS = 32
TOP_K = 4
D_FF = D_MODEL
SWIGLU_LIMIT = 7.0
SWIGLU_ALPHA = 1.702
MOE_BLOCK = 256
DT_MIN = 1e-3
DT_MAX = 1e-1
N_LAYERS_A = (DEPTH + 1) // N_MIXERS
N_LAYERS_B = DEPTH // N_MIXERS

kernel_name = 'hybrid_ssd_s5_moe_block'


def rms_norm(x, w):
    xf = x.astype(jnp.float32)
    y = xf * lax.rsqrt(jnp.mean(xf * xf, axis=-1, keepdims=True) + RMS_EPS)
    return (y * w.astype(jnp.float32)).astype(x.dtype)


def causal_depthwise_conv(u, w, b):
    out = lax.conv_general_dilated(u, w[:, None, :], window_strides=(1,), padding=[(w.shape[0] - 1, 0)],
                                   dimension_numbers=('NWC', 'WIO', 'NWC'), feature_group_count=u.shape[-1])
    return out + b


def ssd_chunked_scan(xdt, a, Bg, Cg):
    bsz, L = xdt.shape[0], xdt.shape[1]
    nc = -(-L // SSD_CHUNK)
    pad = nc * SSD_CHUNK - L

    def to_chunks(t):
        t = jnp.pad(t, [(0, 0), (0, pad)] + [(0, 0)] * (t.ndim - 2))
        t = t.reshape((bsz, nc, SSD_CHUNK) + t.shape[2:])
        return jnp.moveaxis(t, 1, 0)

    xs = (to_chunks(xdt), to_chunks(a), to_chunks(Bg), to_chunks(Cg))
    causal = jnp.tril(jnp.ones((SSD_CHUNK, SSD_CHUNK), dtype=bool))[None, :, :, None, None]

    def step(state, inp):
        x_c, a_c, B_c, C_c = inp
        a_cum = jnp.cumsum(a_c, axis=1)
        seg = a_cum[:, :, None] - a_cum[:, None, :]
        decay = jnp.exp(jnp.where(causal, seg, -jnp.inf))
        cb = jnp.einsum('blgn,bsgn->blsg', C_c, B_c)
        y_diag = jnp.einsum('blsgr,bsgrp->blgrp', cb[..., None] * decay, x_c)
        y_off = jnp.einsum('blgn,bgrpn->blgrp', C_c, state) * jnp.exp(a_cum)[..., None]
        a_tot = a_cum[:, -1]
        w_in = jnp.exp(a_tot[:, None] - a_cum)
        new_state = state * jnp.exp(a_tot)[..., None, None] + jnp.einsum('bsgn,bsgrp->bgrpn', B_c, w_in[..., None] * x_c)
        return new_state, y_diag + y_off

    state0 = jnp.zeros((bsz, SSD_GROUPS, SSD_HEADS_PER_GROUP, SSD_HEAD_DIM, SSD_STATE), jnp.float32)
    _, ys = lax.scan(step, state0, xs)
    ys = jnp.moveaxis(ys, 0, 1).reshape((bsz, nc * SSD_CHUNK) + ys.shape[3:])
    return ys[:, :L]


def ssd_mixer(h, in_proj, conv_w, conv_b, dt_bias, a_log, d_skip, norm_w, out_proj):
    bsz, L, _ = h.shape
    G, R, P, N = SSD_GROUPS, SSD_HEADS_PER_GROUP, SSD_HEAD_DIM, SSD_STATE
    zxbcdt = h @ in_proj
    z, xbc, dt = jnp.split(zxbcdt, [SSD_D_INNER, SSD_D_INNER + SSD_CONV_DIM], axis=-1)
    xbc = jax.nn.silu(causal_depthwise_conv(xbc, conv_w, conv_b))
    xs, Bg, Cg = jnp.split(xbc, [SSD_D_INNER, SSD_D_INNER + G * N], axis=-1)
    xs = xs.astype(jnp.float32).reshape(bsz, L, G, R, P)
    Bg = Bg.astype(jnp.float32).reshape(bsz, L, G, N)
    Cg = Cg.astype(jnp.float32).reshape(bsz, L, G, N)
    dt = jax.nn.softplus(dt.astype(jnp.float32) + dt_bias.astype(jnp.float32)).reshape(bsz, L, G, R)
    A = -jnp.exp(a_log.astype(jnp.float32)).reshape(G, R)
    y = ssd_chunked_scan(xs * dt[..., None], dt * A, Bg, Cg)
    y = y + d_skip.astype(jnp.float32).reshape(G, R, 1) * xs
    y = y.reshape(bsz, L, G, R * P) * jax.nn.silu(z.astype(jnp.float32)).reshape(bsz, L, G, R * P)
    y = y * lax.rsqrt(jnp.mean(y * y, axis=-1, keepdims=True) + RMS_EPS)
    y = y.reshape(bsz, L, SSD_D_INNER) * norm_w.astype(jnp.float32)
    return y.astype(h.dtype) @ out_proj


def _diag_combine(left, right):
    a1, b1 = left
    a2, b2 = right
    return (a1 * a2, a2 * b1 + b2)


def s5_mixer(h, in_proj, lam_re, lam_im, log_step, b_re, b_im, c_re, c_im, d_skip, glu_proj):
    bsz, L, _ = h.shape
    u = (h @ in_proj).astype(jnp.float32)
    ug = u.reshape(bsz, L, S5_GROUPS, S5_GROUP_CH)
    lam = lax.complex(lam_re.astype(jnp.float32), lam_im.astype(jnp.float32))
    step = jnp.exp(log_step.astype(jnp.float32))[:, None]
    lam_bar = jnp.exp(lam * step)
    b = lax.complex(b_re.astype(jnp.float32), b_im.astype(jnp.float32))
    b_bar = ((lam_bar - 1.0) / lam)[..., None] * b
    c = lax.complex(c_re.astype(jnp.float32), c_im.astype(jnp.float32))
    bu = jnp.einsum('gpc,blgc->lbgp', b_bar, ug.astype(jnp.complex64))
    a_elems = jnp.broadcast_to(lam_bar[None, None], (L, 1, S5_GROUPS, S5_STATE))
    _, states = lax.associative_scan(_diag_combine, (a_elems, bu), axis=0)
    y = jnp.einsum('gcp,lbgp->blgc', c, states).real.reshape(bsz, L, S5_WIDTH)
    y = y + d_skip.astype(jnp.float32) * u
    g = jax.nn.gelu(y).astype(h.dtype)
    v, gate = jnp.split(g @ glu_proj, 2, axis=-1)
    return v * jax.nn.sigmoid(gate)


def clamped_swiglu(gu):
    g, u = jnp.split(gu, 2, axis=-1)
    g = jnp.minimum(g, SWIGLU_LIMIT)
    u = jnp.clip(u, -SWIGLU_LIMIT, SWIGLU_LIMIT)
    return (u + 1.0) * (g * jax.nn.sigmoid(SWIGLU_ALPHA * g))


def moe_ffn(h, router_w, router_b, w_gate_up, b_gate_up, w_down, b_down):
    bsz, L, dm = h.shape
    T = bsz * L
    tokens = h.reshape(T, dm)
    logits = (tokens @ router_w).astype(jnp.float32) + router_b.astype(jnp.float32)
    top_logits, top_idx = lax.top_k(logits, TOP_K)
    gates = jax.nn.softmax(top_logits, axis=-1)
    flat_e = top_idx.reshape(-1)
    flat_tok = jnp.repeat(jnp.arange(T, dtype=jnp.int32), TOP_K)
    flat_gate = gates.reshape(-1)
    order = jnp.argsort(flat_e)
    se, stok, sgate = flat_e[order], flat_tok[order], flat_gate[order]
    counts = jnp.bincount(flat_e, length=N_EXPERTS)
    start = jnp.cumsum(counts) - counts
    padded = (counts + MOE_BLOCK - 1) // MOE_BLOCK * MOE_BLOCK
    pend = jnp.cumsum(padded)
    pstart = pend - padded
    dest = pstart[se] + jnp.arange(T * TOP_K, dtype=jnp.int32) - start[se]
    n_blocks = (T * TOP_K + N_EXPERTS * (MOE_BLOCK - 1) + MOE_BLOCK - 1) // MOE_BLOCK
    n_rows = n_blocks * MOE_BLOCK
    row_tok = jnp.full((n_rows,), T, jnp.int32).at[dest].set(stok)
    row_gate = jnp.zeros((n_rows,), jnp.float32).at[dest].set(sgate)
    block_expert = jnp.minimum(jnp.searchsorted(pend, jnp.arange(n_blocks) * MOE_BLOCK, side='right'), N_EXPERTS - 1)
    tokens_pad = jnp.concatenate([tokens, jnp.zeros((1, dm), tokens.dtype)], axis=0)

    def expert_block(args):
        tok_idx, e = args
        xb = tokens_pad[tok_idx]
        gu = xb @ w_gate_up[e] + b_gate_up[e]
        return clamped_swiglu(gu) @ w_down[e] + b_down[e]

    ys = lax.map(expert_block, (row_tok.reshape(n_blocks, MOE_BLOCK), block_expert))
    ys = ys.reshape(n_rows, dm) * row_gate[:, None].astype(ys.dtype)
    out = jnp.zeros((T + 1, dm), ys.dtype).at[row_tok].add(ys)[:T]
    return out.reshape(bsz, L, dm)


def setup_inputs(seed: int = 0) -> dict:
    key = jax.random.key(seed)
    ks = jax.random.split(key, 32)
    f32 = jnp.float32
    nrm = lambda k, s, scale: jax.random.normal(k, s, f32) * scale
    log_dt_lo, log_dt_hi = math.log(DT_MIN), math.log(DT_MAX)
    na, nb = N_LAYERS_A, N_LAYERS_B
    dt0 = jnp.exp(jax.random.uniform(ks[8], (na, SSD_HEADS), f32, log_dt_lo, log_dt_hi))
    return {
        'x': nrm(ks[0], (BATCH, SEQ, D_MODEL), 1.0),
        'norm_mix_w': 1.0 + nrm(ks[1], (DEPTH, D_MODEL), 0.01),
        'norm_ffn_w': 1.0 + nrm(ks[2], (DEPTH, D_MODEL), 0.01),
        'final_norm_w': 1.0 + nrm(ks[3], (D_MODEL,), 0.01),
        'a_in_proj': nrm(ks[4], (na, D_MODEL, SSD_IN_DIM), D_MODEL ** -0.5),
        'a_conv_w': nrm(ks[5], (na, SSD_CONV, SSD_CONV_DIM), SSD_CONV ** -0.5),
        'a_conv_b': nrm(ks[6], (na, SSD_CONV_DIM), 0.01),
        'a_dt_bias': dt0 + jnp.log(-jnp.expm1(-dt0)),
        'a_log': jnp.log(jax.random.uniform(ks[9], (na, SSD_HEADS), f32, 1.0, 16.0)),
        'a_d_skip': 1.0 + nrm(ks[10], (na, SSD_HEADS), 0.01),
        'a_norm_w': 1.0 + nrm(ks[11], (na, SSD_D_INNER), 0.01),
        'a_out_proj': nrm(ks[12], (na, SSD_D_INNER, D_MODEL), SSD_D_INNER ** -0.5),
        'b_in_proj': nrm(ks[13], (nb, D_MODEL, S5_WIDTH), D_MODEL ** -0.5),
        'b_lam_re': -0.5 + nrm(ks[14], (nb, S5_GROUPS, S5_STATE), 0.01),
        'b_lam_im': math.pi * jnp.arange(S5_STATE, dtype=f32) + nrm(ks[15], (nb, S5_GROUPS, S5_STATE), 0.01),
        'b_log_step': jax.random.uniform(ks[16], (nb, S5_GROUPS), f32, log_dt_lo, log_dt_hi),
        'b_b_re': nrm(ks[17], (nb, S5_GROUPS, S5_STATE, S5_GROUP_CH), (2 * S5_GROUP_CH) ** -0.5),
        'b_b_im': nrm(ks[18], (nb, S5_GROUPS, S5_STATE, S5_GROUP_CH), (2 * S5_GROUP_CH) ** -0.5),
        'b_c_re': nrm(ks[19], (nb, S5_GROUPS, S5_GROUP_CH, S5_STATE), S5_STATE ** -0.5),
        'b_c_im': nrm(ks[20], (nb, S5_GROUPS, S5_GROUP_CH, S5_STATE), S5_STATE ** -0.5),
        'b_d_skip': nrm(ks[21], (nb, S5_WIDTH), 1.0),
        'b_glu_proj': nrm(ks[22], (nb, S5_WIDTH, 2 * D_MODEL), S5_WIDTH ** -0.5),
        'router_w': nrm(ks[23], (DEPTH, D_MODEL, N_EXPERTS), D_MODEL ** -0.5),
        'router_b': nrm(ks[24], (DEPTH, N_EXPERTS), 0.01),
        'w_gate_up': nrm(ks[25], (DEPTH, N_EXPERTS, D_MODEL, 2 * D_FF), D_MODEL ** -0.5),
        'b_gate_up': nrm(ks[26], (DEPTH, N_EXPERTS, 2 * D_FF), 0.01),
        'w_down': nrm(ks[27], (DEPTH, N_EXPERTS, D_FF, D_MODEL), D_FF ** -0.5),
        'b_down': nrm(ks[28], (DEPTH, N_EXPERTS, D_MODEL), 0.01),
    }


def reference(x, norm_mix_w, norm_ffn_w, final_norm_w,
              a_in_proj, a_conv_w, a_conv_b, a_dt_bias, a_log, a_d_skip, a_norm_w, a_out_proj,
              b_in_proj, b_lam_re, b_lam_im, b_log_step, b_b_re, b_b_im, b_c_re, b_c_im, b_d_skip, b_glu_proj,
              router_w, router_b, w_gate_up, b_gate_up, w_down, b_down):
    h = x
    for i in range(DEPTH):
        hn = rms_norm(h, norm_mix_w[i])
        j = i // N_MIXERS
        if i % N_MIXERS == 0:
            mix = ssd_mixer(hn, a_in_proj[j], a_conv_w[j], a_conv_b[j], a_dt_bias[j], a_log[j],
                            a_d_skip[j], a_norm_w[j], a_out_proj[j])
        else:
            mix = s5_mixer(hn, b_in_proj[j], b_lam_re[j], b_lam_im[j], b_log_step[j], b_b_re[j], b_b_im[j],
                           b_c_re[j], b_c_im[j], b_d_skip[j], b_glu_proj[j])
        h = h + mix
        h = h + moe_ffn(rms_norm(h, norm_ffn_w[i]), router_w[i], router_b[i], w_gate_up[i], b_gate_up[i],
                        w_down[i], b_down[i])
    return rms_norm(h, final_norm_w)
```

```python
import functools
import math

import jax
import jax.numpy as jnp
from jax import lax
from jax.experimental import pallas as pl
from jax.experimental.pallas import tpu as pltpu

F32 = jnp.float32
BF16 = jnp.bfloat16
HIGHEST = lax.Precision.HIGHEST

RMS_EPS = 1e-6
LANES = 128
SUBLANES = 8
V7X_VMEM_LIMIT_BYTES = 58 * 1024 * 1024

SSD_HEAD_DIM = 64
SSD_GROUPS = 8
SSD_HEADS_PER_GROUP = 8
SSD_STATE = 128
SSD_CONV = 4
SSD_CHUNK = 256
SSD_GROUP_W = SSD_HEADS_PER_GROUP * SSD_HEAD_DIM
SSD_PAIRS = SSD_GROUP_W // LANES

S5_GROUP_CH = 16
S5_STATE = 64
S5_CHUNK = 16
S5_TILE = S5_CHUNK * S5_GROUP_CH

TOP_K = 4
SWIGLU_LIMIT = 7.0
SWIGLU_ALPHA = 1.702
MOE_TM = 1024
MOE_SUB = 256
MOE_TF = 256


def _params(semantics):
    return pltpu.CompilerParams(dimension_semantics=semantics, vmem_limit_bytes=V7X_VMEM_LIMIT_BYTES)


def _sigmoid(x):
    return 1.0 / (1.0 + jnp.exp(-x))


def _rms(x, w):
    ms = jnp.mean(x * x, axis=-1, keepdims=True)
    return x * lax.rsqrt(ms + RMS_EPS) * w


def _rmsnorm_body(x_ref, w_ref, o_ref):
    o_ref[...] = _rms(x_ref[...], w_ref[...]).astype(o_ref.dtype)


def _rmsnorm(x, w, out_dtype, tm=512):
    t, d = x.shape
    tm = min(tm, t)
    return pl.pallas_call(
        _rmsnorm_body,
        grid=(t // tm,),
        in_specs=[pl.BlockSpec((tm, d), lambda i: (i, 0)), pl.BlockSpec((1, d), lambda i: (0, 0))],
        out_specs=pl.BlockSpec((tm, d), lambda i: (i, 0)),
        out_shape=jax.ShapeDtypeStruct((t, d), out_dtype),
        compiler_params=_params(("parallel",)),
        name="rmsnorm",
    )(x, w.reshape(1, d))


def _mm_body(x_ref, w_ref, o_ref):
    o_ref[...] = jnp.dot(x_ref[...], w_ref[...], preferred_element_type=F32).astype(o_ref.dtype)


def _mm_res_body(x_ref, w_ref, r_ref, o_ref):
    o_ref[...] = r_ref[...] + jnp.dot(x_ref[...], w_ref[...], preferred_element_type=F32)


def _mm_glu_body(x_ref, wv_ref, wg_ref, r_ref, o_ref):
    x = x_ref[...]
    v = jnp.dot(x, wv_ref[...], preferred_element_type=F32)
    g = jnp.dot(x, wg_ref[...], preferred_element_type=F32)
    o_ref[...] = r_ref[...] + v * _sigmoid(g)


def _matmul(x, w, out_dtype, tm, tn, name):
    m, k = x.shape
    n = w.shape[1]
    tm, tn = min(tm, m), min(tn, n)
    return pl.pallas_call(
        _mm_body,
        grid=(m // tm, n // tn),
        in_specs=[pl.BlockSpec((tm, k), lambda i, j: (i, 0)), pl.BlockSpec((k, tn), lambda i, j: (0, j))],
        out_specs=pl.BlockSpec((tm, tn), lambda i, j: (i, j)),
        out_shape=jax.ShapeDtypeStruct((m, n), out_dtype),
        compiler_params=_params(("parallel", "parallel")),
        name=name,
    )(x, w)


def _matmul_residual(x, w, res, tm, tn, name):
    m, k = x.shape
    n = w.shape[1]
    tm, tn = min(tm, m), min(tn, n)
    return pl.pallas_call(
        _mm_res_body,
        grid=(m // tm, n // tn),
        in_specs=[pl.BlockSpec((tm, k), lambda i, j: (i, 0)), pl.BlockSpec((k, tn), lambda i, j: (0, j)),
                  pl.BlockSpec((tm, tn), lambda i, j: (i, j))],
        out_specs=pl.BlockSpec((tm, tn), lambda i, j: (i, j)),
        out_shape=jax.ShapeDtypeStruct((m, n), F32),
        compiler_params=_params(("parallel", "parallel")),
        name=name,
    )(x, w, res)


def _matmul_glu_residual(x, w, res, tm, tn, name):
    m, k = x.shape
    n = w.shape[1] // 2
    tm, tn = min(tm, m), min(tn, n)
    nj = n // tn
    return pl.pallas_call(
        _mm_glu_body,
        grid=(m // tm, nj),
        in_specs=[pl.BlockSpec((tm, k), lambda i, j: (i, 0)),
                  pl.BlockSpec((k, tn), lambda i, j: (0, j)),
                  pl.BlockSpec((k, tn), lambda i, j: (0, nj + j)),
                  pl.BlockSpec((tm, tn), lambda i, j: (i, j))],
        out_specs=pl.BlockSpec((tm, tn), lambda i, j: (i, j)),
        out_shape=jax.ShapeDtypeStruct((m, n), F32),
        compiler_params=_params(("parallel", "parallel")),
        name=name,
    )(x, w, w, res)


def _ssd_body(z_ref, x_ref, b_ref, c_ref, dt_ref,
              cwx_ref, cwb_ref, cwc_ref, cbx_ref, cbb_ref, cbc_ref,
              dtb_ref, alog_ref, dexp_ref, nw_ref,
              o_ref,
              tx_ref, tb_ref, tc_ref, st_ref, pad_ref):
    c = pl.program_id(1)
    g = pl.program_id(2)
    lc = x_ref.shape[1]

    @pl.when(c == 0)
    def _():
        tx_ref[g] = jnp.zeros(tx_ref.shape[1:], F32)
        tb_ref[g] = jnp.zeros(tb_ref.shape[1:], F32)
        tc_ref[g] = jnp.zeros(tc_ref.shape[1:], F32)
        st_ref[g] = jnp.zeros(st_ref.shape[1:], F32)

    def conv_silu(u_ref, w_ref, bias_ref, tail_ref):
        width = u_ref.shape[2]
        u = u_ref[0].astype(F32)
        pad_ref[0:SUBLANES, 0:width] = tail_ref[g]
        pad_ref[SUBLANES:SUBLANES + lc, 0:width] = u
        tail_ref[g] = u[lc - SUBLANES:lc, :]
        acc = bias_ref[...] + w_ref[SSD_CONV - 1:SSD_CONV, :] * u
        for k in range(SSD_CONV - 1):
            off = SUBLANES - (SSD_CONV - 1) + k
            acc = acc + w_ref[k:k + 1, :] * pad_ref[off:off + lc, 0:width]
        return acc * _sigmoid(acc)

    xg = conv_silu(x_ref, cwx_ref, cbx_ref, tx_ref)
    bg = conv_silu(b_ref, cwb_ref, cbb_ref, tb_ref)
    cg = conv_silu(c_ref, cwc_ref, cbc_ref, tc_ref)

    dt_in = dt_ref[0] + dtb_ref[0]
    dt = jnp.maximum(dt_in, 0.0) + jnp.log1p(jnp.exp(-jnp.abs(dt_in)))
    a = dt * (-jnp.exp(alog_ref[0]))
    rows = lax.broadcasted_iota(jnp.int32, (lc, lc), 0)
    cols = lax.broadcasted_iota(jnp.int32, (lc, lc), 1)
    tril = rows >= cols
    a_cum = jnp.dot(tril.astype(F32), a, precision=HIGHEST, preferred_element_type=F32)
    a_cum_t = a_cum.T
    dt_t = dt.T

    bgt = bg.T
    cb = jnp.dot(cg.astype(BF16), bgt.astype(BF16), preferred_element_type=F32)
    left = lax.broadcasted_iota(jnp.int32, (1, LANES), 1) < SSD_HEAD_DIM
    neg_inf = jnp.float32(-jnp.inf)

    ys = []
    for p in range(SSD_PAIRS):
        xp = xg[:, p * LANES:(p + 1) * LANES]
        xpb = xp.astype(BF16)
        st = st_ref[g, p]
        rhs = jnp.concatenate([xpb, st.astype(BF16)], axis=0)
        y_h, st_h = [], []
        for hh in range(2):
            r = 2 * p + hh
            colb = jnp.broadcast_to(a_cum[:, r:r + 1], (lc, LANES))
            rowb = a_cum_t[r:r + 1, :]
            dtr = dt_t[r:r + 1, :]
            a_tot = a_cum_t[r:r + 1, lc - 1:lc]
            seg = jnp.concatenate([colb] * (lc // LANES), axis=1) - rowb
            decay = jnp.exp(jnp.where(tril, seg, neg_inf))
            m = (cb * decay * dtr).astype(BF16)
            cs = (cg * jnp.exp(colb)).astype(BF16)
            lhs = jnp.concatenate([m, cs], axis=1)
            y_h.append(jnp.dot(lhs, rhs, preferred_element_type=F32))
            w_row = jnp.exp(a_tot - rowb) * dtr
            s_new = jnp.dot((bgt * w_row).astype(BF16), xpb, preferred_element_type=F32)
            st_h.append(st * jnp.exp(a_tot) + s_new)
        ys.append(jnp.where(left, y_h[0], y_h[1]) + dexp_ref[:, p * LANES:(p + 1) * LANES] * xp)
        st_ref[g, p] = jnp.where(left, st_h[0], st_h[1])

    y = jnp.concatenate(ys, axis=1)
    z = z_ref[0].astype(F32)
    yz = y * (z * _sigmoid(z))
    ms = jnp.mean(yz * yz, axis=-1, keepdims=True)
    o_ref[0] = (yz * lax.rsqrt(ms + RMS_EPS) * nw_ref[...]).astype(o_ref.dtype)


def _ssd(zx, dt, conv_w, conv_b, dt_bias, a_log, d_skip, norm_w):
    bsz, seq, _ = zx.shape
    g_n, gw, n = SSD_GROUPS, SSD_GROUP_W, SSD_STATE
    d_inner = g_n * gw
    lc = min(SSD_CHUNK, seq)
    nc = seq // lc
    xb0 = d_inner // gw
    bb0 = 2 * d_inner // n
    cb0 = bb0 + g_n
    cwx, cwb, cwc = conv_w[:, :d_inner], conv_w[:, d_inner:d_inner + g_n * n], conv_w[:, d_inner + g_n * n:]
    cb_ = conv_b.reshape(1, -1)
    cbx, cbb, cbc = cb_[:, :d_inner], cb_[:, d_inner:d_inner + g_n * n], cb_[:, d_inner + g_n * n:]
    pad_heads = lambda v: jnp.pad(v.reshape(g_n, 1, SSD_HEADS_PER_GROUP), ((0, 0), (0, 0), (0, LANES - SSD_HEADS_PER_GROUP)))
    dexp = jnp.repeat(d_skip, SSD_HEAD_DIM).reshape(1, d_inner)
    grp = lambda w: pl.BlockSpec((1, lc, w), lambda b, c, g: (b, c, g))
    return pl.pallas_call(
        _ssd_body,
        grid=(bsz, nc, g_n),
        in_specs=[
            pl.BlockSpec((1, lc, gw), lambda b, c, g: (b, c, g)),
            pl.BlockSpec((1, lc, gw), lambda b, c, g: (b, c, xb0 + g)),
            pl.BlockSpec((1, lc, n), lambda b, c, g: (b, c, bb0 + g)),
            pl.BlockSpec((1, lc, n), lambda b, c, g: (b, c, cb0 + g)),
            pl.BlockSpec((1, lc, LANES), lambda b, c, g: (b, c, g)),
            pl.BlockSpec((SSD_CONV, gw), lambda b, c, g: (0, g)),
            pl.BlockSpec((SSD_CONV, n), lambda b, c, g: (0, g)),
            pl.BlockSpec((SSD_CONV, n), lambda b, c, g: (0, g)),
            pl.BlockSpec((1, gw), lambda b, c, g: (0, g)),
            pl.BlockSpec((1, n), lambda b, c, g: (0, g)),
            pl.BlockSpec((1, n), lambda b, c, g: (0, g)),
            pl.BlockSpec((1, 1, LANES), lambda b, c, g: (g, 0, 0)),
            pl.BlockSpec((1, 1, LANES), lambda b, c, g: (g, 0, 0)),
            pl.BlockSpec((1, gw), lambda b, c, g: (0, g)),
            pl.BlockSpec((1, gw), lambda b, c, g: (0, g)),
        ],
        out_specs=pl.BlockSpec((1, lc, gw), lambda b, c, g: (b, c, g)),
        out_shape=jax.ShapeDtypeStruct((bsz, seq, d_inner), BF16),
        scratch_shapes=[
            pltpu.VMEM((g_n, SUBLANES, gw), F32),
            pltpu.VMEM((g_n, SUBLANES, n), F32),
            pltpu.VMEM((g_n, SUBLANES, n), F32),
            pltpu.VMEM((g_n, SSD_PAIRS, n, LANES), F32),
            pltpu.VMEM((SUBLANES + lc, gw), F32),
        ],
        compiler_params=_params(("arbitrary", "arbitrary", "arbitrary")),
        name="ssd_scan",
    )(zx, zx, zx, zx, dt, cwx, cwb, cwc, cbx, cbb, cbc, pad_heads(dt_bias), pad_heads(a_log), dexp,
      norm_w.reshape(1, d_inner))


def _ssd_mixer(h, xn, in_proj, conv_w, conv_b, dt_bias, a_log, d_skip, norm_w, out_proj, bsz, seq):
    t, d = h.shape
    n_heads = dt_bias.shape[0]
    d_main = in_proj.shape[1] - n_heads
    w_main = in_proj[:, :d_main].astype(BF16)
    w_dt = jnp.pad(in_proj[:, d_main:], ((0, 0), (0, LANES - n_heads))).astype(BF16)
    zx = _matmul(xn, w_main, BF16, 1024, 1024, "ssd_in_proj")
    dt_raw = _matmul(xn, w_dt, F32, 1024, LANES, "ssd_dt_proj")
    dt = dt_raw[:, :n_heads].reshape(t, SSD_GROUPS, SSD_HEADS_PER_GROUP)
    dt = jnp.pad(dt, ((0, 0), (0, 0), (0, LANES - SSD_HEADS_PER_GROUP))).reshape(bsz, seq, SSD_GROUPS * LANES)
    y = _ssd(zx.reshape(bsz, seq, d_main), dt, conv_w, conv_b, dt_bias, a_log, d_skip, norm_w)
    return _matmul_residual(y.reshape(t, -1), out_proj.astype(BF16), h, 512, 1024, "ssd_out_proj")


def _s5_body(u_ref, m_ref, wre_ref, wim_ref, ore_ref, oim_ref, are_ref, aim_ref, o_ref, *, n_chunks):
    r = u_ref.shape[1]
    u0, u1 = u_ref[0], u_ref[1]
    xre = (jnp.dot(u0, wre_ref[0], preferred_element_type=F32) + jnp.dot(u1, wre_ref[1], preferred_element_type=F32))
    xim = (jnp.dot(u0, wim_ref[0], preferred_element_type=F32) + jnp.dot(u1, wim_ref[1], preferred_element_type=F32))
    kidx = lax.broadcasted_iota(jnp.int32, (r, LANES), 0) % n_chunks

    def shifted(v, sh):
        return jnp.where(kidx >= sh, pltpu.roll(v, sh, 0), 0.0)

    sh, lvl = 1, 0
    while sh < n_chunks:
        are, aim = are_ref[0, lvl:lvl + 1, :], aim_ref[0, lvl:lvl + 1, :]
        sre, sim = shifted(xre, sh), shifted(xim, sh)
        xre, xim = xre + are * sre - aim * sim, xim + are * sim + aim * sre
        sh, lvl = sh * 2, lvl + 1
    pre = shifted(xre, 1).astype(BF16)
    pim = shifted(xim, 1).astype(BF16)
    for i, u in enumerate((u0, u1)):
        y = (jnp.dot(u, m_ref[i], preferred_element_type=F32)
             + jnp.dot(pre, ore_ref[i], preferred_element_type=F32)
             + jnp.dot(pim, oim_ref[i], preferred_element_type=F32))
        gelu = 0.5 * y * (1.0 + jnp.tanh(math.sqrt(2.0 / math.pi) * (y + 0.044715 * (y * y * y))))
        o_ref[i] = gelu.astype(o_ref.dtype)


def _s5_tables(lam_re, lam_im, log_step, b_re, b_im, c_re, c_im, d_skip, n_chunks):
    g_n, p_n = lam_re.shape
    ch, lcs = S5_GROUP_CH, S5_CHUNK
    lam = lax.complex(lam_re, lam_im)
    step = jnp.exp(log_step)[:, None]
    lam_bar = jnp.exp(lam * step)
    b_bar = ((lam_bar - 1.0) / lam)[..., None] * lax.complex(b_re, b_im)
    cc = lax.complex(c_re, c_im)
    taus = jnp.arange(lcs + 1, dtype=F32)
    lam_pow = jnp.exp((lam * step)[None] * taus[:, None, None])
    kern = jnp.einsum('gcp,tgp,gpd->gtcd', cc, lam_pow[:lcs], b_bar, precision=HIGHEST).real
    s_i = jnp.arange(lcs)[:, None]
    t_i = jnp.arange(lcs)[None, :]
    lag = jnp.clip(t_i - s_i, 0, lcs - 1)
    m = kern[:, lag]
    m = jnp.where((t_i >= s_i)[None, :, :, None, None], m, 0.0)
    m = jnp.transpose(m, (0, 1, 4, 2, 3))
    eye = (s_i == t_i)[None, :, None, :, None] & (jnp.arange(ch)[:, None] == jnp.arange(ch)[None, :])[None, None, :, None, :]
    m = m + jnp.where(eye, d_skip.reshape(g_n, 1, 1, 1, ch), 0.0)
    m = m.reshape(g_n, S5_TILE, S5_TILE)
    win = jnp.einsum('sgp,gpd->gsdp', lam_pow[:lcs][::-1], b_bar)
    win = win.reshape(g_n, S5_TILE, p_n)
    wout = jnp.einsum('gcp,tgp->gptc', cc, lam_pow[1:lcs + 1]).reshape(g_n, p_n, S5_TILE)
    odd = (jnp.arange(g_n) % 2 == 1)[:, None, None]
    zc = jnp.zeros((g_n, S5_TILE, p_n), F32)
    place_cols = lambda w: jnp.where(odd, jnp.concatenate([zc, w], axis=2), jnp.concatenate([w, zc], axis=2))
    zr = jnp.zeros((g_n, p_n, S5_TILE), F32)
    place_rows = lambda w: jnp.where(odd, jnp.concatenate([zr, w], axis=1), jnp.concatenate([w, zr], axis=1))
    n_lvl = max(1, int(math.ceil(math.log2(max(n_chunks, 2)))))
    lvl_pow = (lcs * (2.0 ** jnp.arange(n_lvl, dtype=F32)))
    a_pow = jnp.exp((lam * step)[None] * lvl_pow[:, None, None])
    a_pow = jnp.transpose(a_pow, (1, 0, 2)).reshape(g_n // 2, 2, n_lvl, p_n)
    a_pow = jnp.transpose(a_pow, (0, 2, 1, 3)).reshape(g_n // 2, n_lvl, 2 * p_n)
    a_pad = ((0, 0), (0, SUBLANES - n_lvl % SUBLANES if n_lvl % SUBLANES else 0), (0, 0))
    return (m.astype(BF16), place_cols(win.real).astype(BF16), place_cols(win.imag).astype(BF16),
            place_rows(wout.real).astype(BF16), place_rows(-wout.imag).astype(BF16),
            jnp.pad(a_pow.real, a_pad), jnp.pad(a_pow.imag, a_pad))


def _s5_mixer(h, xn, in_proj, lam_re, lam_im, log_step, b_re, b_im, c_re, c_im, d_skip, glu_proj, bsz, seq):
    t, d = h.shape
    g_n = lam_re.shape[0]
    width = g_n * S5_GROUP_CH
    n_chunks = seq // S5_CHUNK
    rows = bsz * n_chunks
    u = _matmul(xn, in_proj.astype(BF16), BF16, 1024, 1024, "s5_in_proj")
    ut = jnp.transpose(u.reshape(bsz, n_chunks, S5_CHUNK, g_n, S5_GROUP_CH), (3, 0, 1, 2, 4)).reshape(g_n, rows, S5_TILE)
    m, wre, wim, ore, oim, are, aim = _s5_tables(lam_re, lam_im, log_step, b_re, b_im, c_re, c_im, d_skip, n_chunks)
    n_lvl_pad = are.shape[1]
    pair3 = lambda a, b: pl.BlockSpec((2, a, b), lambda i: (i, 0, 0))
    gt = pl.pallas_call(
        functools.partial(_s5_body, n_chunks=n_chunks),
        grid=(g_n // 2,),
        in_specs=[pair3(rows, S5_TILE), pair3(S5_TILE, S5_TILE), pair3(S5_TILE, 2 * S5_STATE), pair3(S5_TILE, 2 * S5_STATE),
                  pair3(2 * S5_STATE, S5_TILE), pair3(2 * S5_STATE, S5_TILE),
                  pl.BlockSpec((1, n_lvl_pad, 2 * S5_STATE), lambda i: (i, 0, 0)),
                  pl.BlockSpec((1, n_lvl_pad, 2 * S5_STATE), lambda i: (i, 0, 0))],
        out_specs=pair3(rows, S5_TILE),
        out_shape=jax.ShapeDtypeStruct((g_n, rows, S5_TILE), BF16),
        compiler_params=_params(("parallel",)),
        name="s5_scan",
    )(ut, m, wre, wim, ore, oim, are, aim)
    gact = jnp.transpose(gt.reshape(g_n, bsz, n_chunks, S5_CHUNK, S5_GROUP_CH), (1, 2, 3, 0, 4)).reshape(t, width)
    return _matmul_glu_residual(gact, glu_proj.astype(BF16), h, 1024, 1024, "s5_glu_proj")


def _router_body(h_ref, nw_ref, rw_ref, rb_ref, idx_ref, gate_ref, *, n_experts):
    xn = _rms(h_ref[...], nw_ref[...])
    logits = jnp.dot(xn, rw_ref[...], precision=HIGHEST, preferred_element_type=F32) + rb_ref[...]
    tm = logits.shape[0]
    lane = lax.broadcasted_iota(jnp.int32, (tm, LANES), 1)
    lane_f = lane.astype(F32)
    neg_inf = jnp.float32(-jnp.inf)
    cur = jnp.where(lane < n_experts, logits, neg_inf)
    tops, idxs = [], []
    for _ in range(TOP_K):
        mx = jnp.max(cur, axis=-1, keepdims=True)
        ix = jnp.min(jnp.where(cur == mx, lane_f, float(LANES)), axis=-1, keepdims=True)
        cur = jnp.where(lane_f == ix, neg_inf, cur)
        tops.append(mx)
        idxs.append(ix)
    es = [jnp.exp(v - tops[0]) for v in tops]
    denom = es[0] + es[1] + es[2] + es[3]
    idx_out = jnp.zeros((tm, LANES), F32)
    gate_out = jnp.zeros((tm, LANES), F32)
    for k in range(TOP_K):
        idx_out = jnp.where(lane == k, idxs[k], idx_out)
        gate_out = jnp.where(lane == k, es[k] / denom, gate_out)
    idx_ref[...] = idx_out.astype(jnp.int32)
    gate_ref[...] = gate_out


def _rank_body(idx_ref, rank_ref, cnt_ref, carry_ref):
    i = pl.program_id(0)
    tb = idx_ref.shape[0]

    @pl.when(i == 0)
    def _():
        carry_ref[...] = jnp.zeros(carry_ref.shape, F32)

    idx = idx_ref[...]
    lane = lax.broadcasted_iota(jnp.int32, (tb, LANES), 1)
    onehots = [(lane == idx[:, k:k + 1]) for k in range(TOP_K)]
    hits = jnp.zeros((tb, LANES), F32)
    for oh in onehots:
        hits = hits + oh.astype(F32)
    rows = lax.broadcasted_iota(jnp.int32, (tb, tb), 0)
    cols = lax.broadcasted_iota(jnp.int32, (tb, tb), 1)
    strict = (rows > cols).astype(BF16)
    before = jnp.dot(strict, hits.astype(BF16), preferred_element_type=F32) + carry_ref[0:1, :]
    out = jnp.zeros((tb, LANES), jnp.int32)
    for k, oh in enumerate(onehots):
        rk = jnp.sum(jnp.where(oh, before, 0.0), axis=-1, keepdims=True)
        out = jnp.where(lane == k, rk.astype(jnp.int32), out)
    rank_ref[...] = out
    total = carry_ref[0:1, :] + jnp.sum(hits, axis=0, keepdims=True)
    carry_ref[...] = jnp.broadcast_to(total, carry_ref.shape)
    cnt_ref[...] = jnp.broadcast_to(total, cnt_ref.shape)


def _dispatch_body(dest_ref, h_ref, nw_ref, xs_in_ref, xs_ref, buf_ref, sem):
    del xs_in_ref
    tt = h_ref.shape[0]
    buf_ref[...] = _rms(h_ref[...], nw_ref[...])

    def row_copy(i, k):
        d = dest_ref[0, 0, i * TOP_K + k]
        return pltpu.make_async_copy(buf_ref.at[pl.ds(i, 1), :], xs_ref.at[pl.ds(d, 1), :], sem)

    def start(i, carry):
        for k in range(TOP_K):
            row_copy(i, k).start()
        return carry

    def wait(i, carry):
        for k in range(TOP_K):
            row_copy(i, k).wait()
        return carry

    lax.fori_loop(0, tt, start, 0)
    lax.fori_loop(0, tt, wait, 0)


def _combine_body(dest_ref, h_ref, gate_ref, ys_ref, o_ref, buf_ref, sem):
    tt = h_ref.shape[0]

    def row_copy(i, k):
        d = dest_ref[0, 0, i * TOP_K + k]
        return pltpu.make_async_copy(ys_ref.at[pl.ds(d, 1), :], buf_ref.at[k, pl.ds(i, 1), :], sem)

    def start(i, carry):
        for k in range(TOP_K):
            row_copy(i, k).start()
        return carry

    def wait(i, carry):
        for k in range(TOP_K):
            row_copy(i, k).wait()
        return carry

    lax.fori_loop(0, tt, start, 0)
    lax.fori_loop(0, tt, wait, 0)
    gates = gate_ref[...]
    acc = h_ref[...]
    for k in range(TOP_K):
        acc = acc + gates[:, k:k + 1] * buf_ref[k]
    o_ref[...] = acc


def _expert_body(be_ref, ns_ref, nu_ref, x_ref, wg_ref, wu_ref, wd_ref, bg_ref, bu_ref, bd_ref, o_ref):
    del be_ref, nu_ref
    b = pl.program_id(0)
    f = pl.program_id(1)
    n_sub = ns_ref[b]

    @pl.when(f == 0)
    def _():
        o_ref[...] = jnp.zeros(o_ref.shape, F32)

    @pl.when(n_sub > 0)
    def _():
        wg = wg_ref[0].astype(BF16)
        wu = wu_ref[0].astype(BF16)
        wd = wd_ref[0].astype(BF16)
        for s in range(MOE_TM // MOE_SUB):
            @pl.when(s < n_sub)
            def _():
                sl = pl.ds(s * MOE_SUB, MOE_SUB)
                x = x_ref[sl, :].astype(BF16)
                gate = jnp.dot(x, wg, preferred_element_type=F32) + bg_ref[0]
                up = jnp.dot(x, wu, preferred_element_type=F32) + bu_ref[0]
                gate = jnp.minimum(gate, SWIGLU_LIMIT)
                up = jnp.clip(up, -SWIGLU_LIMIT, SWIGLU_LIMIT)
                act = (up + 1.0) * (gate * _sigmoid(SWIGLU_ALPHA * gate))
                part = jnp.dot(act.astype(BF16), wd, preferred_element_type=F32)
                bias = jnp.where(f == 0, 1.0, 0.0) * bd_ref[0]
                o_ref[sl, :] = o_ref[sl, :] + part + bias


def _moe(h, norm_w, router_w, router_b, w_gate_up, b_gate_up, w_down, b_down):
    t, d = h.shape
    n_e = router_w.shape[1]
    d_ff = w_down.shape[1]
    nw = norm_w.reshape(1, d)
    tr = min(512, t)
    rw = jnp.pad(router_w, ((0, 0), (0, LANES - n_e)))
    rb = jnp.pad(router_b, (0, LANES - n_e)).reshape(1, LANES)
    idx, gates = pl.pallas_call(
        functools.partial(_router_body, n_experts=n_e),
        grid=(t // tr,),
        in_specs=[pl.BlockSpec((tr, d), lambda i: (i, 0)), pl.BlockSpec((1, d), lambda i: (0, 0)),
                  pl.BlockSpec((d, LANES), lambda i: (0, 0)), pl.BlockSpec((1, LANES), lambda i: (0, 0))],
        out_specs=[pl.BlockSpec((tr, LANES), lambda i: (i, 0)), pl.BlockSpec((tr, LANES), lambda i: (i, 0))],
        out_shape=[jax.ShapeDtypeStruct((t, LANES), jnp.int32), jax.ShapeDtypeStruct((t, LANES), F32)],
        compiler_params=_params(("parallel",)),
        name="moe_router",
    )(h, nw, rw, rb)
    tb = min(512, t)
    rank, cnt = pl.pallas_call(
        _rank_body,
        grid=(t // tb,),
        in_specs=[pl.BlockSpec((tb, LANES), lambda i: (i, 0))],
        out_specs=[pl.BlockSpec((tb, LANES), lambda i: (i, 0)), pl.BlockSpec((SUBLANES, LANES), lambda i: (0, 0))],
        out_shape=[jax.ShapeDtypeStruct((t, LANES), jnp.int32), jax.ShapeDtypeStruct((SUBLANES, LANES), F32)],
        scratch_shapes=[pltpu.VMEM((SUBLANES, LANES), F32)],
        compiler_params=_params(("arbitrary",)),
        name="moe_rank",
    )(idx)
    counts = cnt[0, :n_e].astype(jnp.int32)
    n_blocks = (t * TOP_K) // MOE_TM + n_e
    blocks_e = (counts + MOE_TM - 1) // MOE_TM
    blk_end = jnp.cumsum(blocks_e)
    blk_start = blk_end - blocks_e
    n_used = blk_end[-1]
    bid = jnp.arange(n_blocks, dtype=jnp.int32)
    blk_expert = jnp.minimum(jnp.searchsorted(blk_end, bid, side='right'), n_e - 1).astype(jnp.int32)
    last_used_expert = blk_expert[jnp.maximum(n_used - 1, 0)]
    blk_expert = jnp.where(bid < n_used, blk_expert, last_used_expert)
    rows_in_blk = jnp.clip(counts[blk_expert] - (bid - blk_start[blk_expert]) * MOE_TM, 0, MOE_TM)
    n_sub = jnp.where(bid < n_used, (rows_in_blk + MOE_SUB - 1) // MOE_SUB, 0).astype(jnp.int32)
    dest = blk_start[idx[:, :TOP_K]] * MOE_TM + rank[:, :TOP_K]
    tt = min(256, t)
    dest3 = dest.reshape(t // tt, 1, tt * TOP_K)
    n_rows = n_blocks * MOE_TM
    xs = pl.pallas_call(
        _dispatch_body,
        grid=(t // tt,),
        in_specs=[pl.BlockSpec((1, 1, tt * TOP_K), lambda i: (i, 0, 0), memory_space=pltpu.SMEM),
                  pl.BlockSpec((tt, d), lambda i: (i, 0)), pl.BlockSpec((1, d), lambda i: (0, 0)),
                  pl.BlockSpec(memory_space=pl.ANY)],
        out_specs=pl.BlockSpec(memory_space=pl.ANY),
        out_shape=jax.ShapeDtypeStruct((n_rows, d), F32),
        scratch_shapes=[pltpu.VMEM((tt, d), F32), pltpu.SemaphoreType.DMA],
        input_output_aliases={3: 0},
        compiler_params=_params(("arbitrary",)),
        name="moe_dispatch",
    )(dest3, h, nw, jnp.zeros((n_rows, d), F32))
    nf = d_ff // MOE_TF
    last_f = nf - 1
    used = lambda b, f, nu: jnp.where(b < nu[0], f, last_f)
    ys = pl.pallas_call(
        _expert_body,
        grid_spec=pltpu.PrefetchScalarGridSpec(
            num_scalar_prefetch=3,
            grid=(n_blocks, nf),
            in_specs=[
                pl.BlockSpec((MOE_TM, d), lambda b, f, be, ns, nu: (jnp.minimum(b, jnp.maximum(nu[0] - 1, 0)), 0)),
                pl.BlockSpec((1, d, MOE_TF), lambda b, f, be, ns, nu: (be[b], 0, used(b, f, nu))),
                pl.BlockSpec((1, d, MOE_TF), lambda b, f, be, ns, nu: (be[b], 0, nf + used(b, f, nu))),
                pl.BlockSpec((1, MOE_TF, d), lambda b, f, be, ns, nu: (be[b], used(b, f, nu), 0)),
                pl.BlockSpec((1, 1, MOE_TF), lambda b, f, be, ns, nu: (be[b], 0, used(b, f, nu))),
                pl.BlockSpec((1, 1, MOE_TF), lambda b, f, be, ns, nu: (be[b], 0, nf + used(b, f, nu))),
                pl.BlockSpec((1, 1, d), lambda b, f, be, ns, nu: (be[b], 0, 0)),
            ],
            out_specs=pl.BlockSpec((MOE_TM, d), lambda b, f, be, ns, nu: (b, 0)),
        ),
        out_shape=jax.ShapeDtypeStruct((n_rows, d), F32),
        compiler_params=_params(("arbitrary", "arbitrary")),
        name="moe_experts",
    )(blk_expert, n_sub, n_used.reshape(1).astype(jnp.int32), xs, w_gate_up, w_gate_up, w_down,
      b_gate_up.reshape(n_e, 1, 2 * d_ff), b_gate_up.reshape(n_e, 1, 2 * d_ff), b_down.reshape(n_e, 1, d))
    tc = min(128, t)
    destc = dest.reshape(t // tc, 1, tc * TOP_K)
    return pl.pallas_call(
        _combine_body,
        grid=(t // tc,),
        in_specs=[pl.BlockSpec((1, 1, tc * TOP_K), lambda i: (i, 0, 0), memory_space=pltpu.SMEM),
                  pl.BlockSpec((tc, d), lambda i: (i, 0)), pl.BlockSpec((tc, LANES), lambda i: (i, 0)),
                  pl.BlockSpec(memory_space=pl.ANY)],
        out_specs=pl.BlockSpec((tc, d), lambda i: (i, 0)),
        out_shape=jax.ShapeDtypeStruct((t, d), F32),
        scratch_shapes=[pltpu.VMEM((TOP_K, tc, d), F32), pltpu.SemaphoreType.DMA],
        compiler_params=_params(("arbitrary",)),
        name="moe_combine",
    )(destc, h, gates, ys)


def kernel(x, norm_mix_w, norm_ffn_w, final_norm_w, a_in_proj, a_conv_w, a_conv_b, a_dt_bias, a_log, a_d_skip, a_norm_w, a_out_proj, b_in_proj, b_lam_re, b_lam_im, b_log_step, b_b_re, b_b_im, b_c_re, b_c_im, b_d_skip, b_glu_proj, router_w, router_b, w_gate_up, b_gate_up, w_down, b_down):
    bsz, seq, d = x.shape
    depth = norm_mix_w.shape[0]
    h = x.reshape(bsz * seq, d)
    for i in range(depth):
        xn = _rmsnorm(h, norm_mix_w[i], BF16)
        j = i // 2
        if i % 2 == 0:
            h = _ssd_mixer(h, xn, a_in_proj[j], a_conv_w[j], a_conv_b[j], a_dt_bias[j], a_log[j], a_d_skip[j],
                           a_norm_w[j], a_out_proj[j], bsz, seq)
        else:
            h = _s5_mixer(h, xn, b_in_proj[j], b_lam_re[j], b_lam_im[j], b_log_step[j], b_b_re[j], b_b_im[j],
                          b_c_re[j], b_c_im[j], b_d_skip[j], b_glu_proj[j], bsz, seq)
        h = _moe(h, norm_ffn_w[i], router_w[i], router_b[i], w_gate_up[i], b_gate_up[i], w_down[i], b_down[i])
    return _rmsnorm(h, final_norm_w, F32).reshape(bsz, seq, d)
```

```python
import functools
import math

import jax
import jax.numpy as jnp
from jax import lax
from jax.experimental import pallas as pl
from jax.experimental.pallas import tpu as pltpu

F32 = jnp.float32
BF16 = jnp.bfloat16
HIGHEST = lax.Precision.HIGHEST

RMS_EPS = 1e-6
LANES = 128
SUBLANES = 8
V7X_VMEM_LIMIT_BYTES = 58 * 1024 * 1024

SSD_HEAD_DIM = 64
SSD_GROUPS = 8
SSD_HEADS_PER_GROUP = 8
SSD_STATE = 128
SSD_CONV = 4
SSD_CHUNK = 256
SSD_GROUP_W = SSD_HEADS_PER_GROUP * SSD_HEAD_DIM
SSD_PAIRS = SSD_GROUP_W // LANES

S5_GROUP_CH = 16
S5_STATE = 64
S5_CHUNK = 16
S5_OCT = LANES // S5_GROUP_CH

TOP_K = 4
SWIGLU_LIMIT = 7.0
SWIGLU_ALPHA = 1.702
MOE_TM = 1024
MOE_SUB = 256
MOE_TF = 256


def _params(semantics):
    return pltpu.CompilerParams(dimension_semantics=semantics, vmem_limit_bytes=V7X_VMEM_LIMIT_BYTES)


def _sigmoid(x):
    return 1.0 / (1.0 + jnp.exp(-x))


def _rms(x, w):
    ms = jnp.mean(x * x, axis=-1, keepdims=True)
    return x * lax.rsqrt(ms + RMS_EPS) * w


def _rmsnorm_body(x_ref, w_ref, o_ref):
    o_ref[...] = _rms(x_ref[...], w_ref[...]).astype(o_ref.dtype)


def _rmsnorm(x, w, out_dtype, tm=512):
    t, d = x.shape
    tm = min(tm, t)
    return pl.pallas_call(
        _rmsnorm_body,
        grid=(t // tm,),
        in_specs=[pl.BlockSpec((tm, d), lambda i: (i, 0)), pl.BlockSpec((1, d), lambda i: (0, 0))],
        out_specs=pl.BlockSpec((tm, d), lambda i: (i, 0)),
        out_shape=jax.ShapeDtypeStruct((t, d), out_dtype),
        compiler_params=_params(("parallel",)),
        name="rmsnorm",
    )(x, w.reshape(1, d))


def _mm_body(x_ref, w_ref, o_ref):
    o_ref[...] = jnp.dot(x_ref[...], w_ref[...], preferred_element_type=F32).astype(o_ref.dtype)


def _mm_res_body(x_ref, w_ref, r_ref, o_ref):
    o_ref[...] = r_ref[...] + jnp.dot(x_ref[...], w_ref[...], preferred_element_type=F32)


def _mm_glu_body(x_ref, wv_ref, wg_ref, r_ref, o_ref):
    x = x_ref[...].astype(BF16)
    v = jnp.dot(x, wv_ref[...], preferred_element_type=F32)
    g = jnp.dot(x, wg_ref[...], preferred_element_type=F32)
    o_ref[...] = r_ref[...] + v * _sigmoid(g)


def _matmul(x, w, out_dtype, tm, tn, name):
    m, k = x.shape
    n = w.shape[1]
    tm, tn = min(tm, m), min(tn, n)
    return pl.pallas_call(
        _mm_body,
        grid=(m // tm, n // tn),
        in_specs=[pl.BlockSpec((tm, k), lambda i, j: (i, 0)), pl.BlockSpec((k, tn), lambda i, j: (0, j))],
        out_specs=pl.BlockSpec((tm, tn), lambda i, j: (i, j)),
        out_shape=jax.ShapeDtypeStruct((m, n), out_dtype),
        compiler_params=_params(("parallel", "parallel")),
        name=name,
    )(x, w)


def _matmul_residual(x, w, res, tm, tn, name):
    m, k = x.shape
    n = w.shape[1]
    tm, tn = min(tm, m), min(tn, n)
    return pl.pallas_call(
        _mm_res_body,
        grid=(m // tm, n // tn),
        in_specs=[pl.BlockSpec((tm, k), lambda i, j: (i, 0)), pl.BlockSpec((k, tn), lambda i, j: (0, j)),
                  pl.BlockSpec((tm, tn), lambda i, j: (i, j))],
        out_specs=pl.BlockSpec((tm, tn), lambda i, j: (i, j)),
        out_shape=jax.ShapeDtypeStruct((m, n), F32),
        compiler_params=_params(("parallel", "parallel")),
        name=name,
    )(x, w, res)


def _matmul_glu_residual(x, w, res, tm, tn, name):
    m, k = x.shape
    n = w.shape[1] // 2
    tm, tn = min(tm, m), min(tn, n)
    nj = n // tn
    return pl.pallas_call(
        _mm_glu_body,
        grid=(m // tm, nj),
        in_specs=[pl.BlockSpec((tm, k), lambda i, j: (i, 0)),
                  pl.BlockSpec((k, tn), lambda i, j: (0, j)),
                  pl.BlockSpec((k, tn), lambda i, j: (0, nj + j)),
                  pl.BlockSpec((tm, tn), lambda i, j: (i, j))],
        out_specs=pl.BlockSpec((tm, tn), lambda i, j: (i, j)),
        out_shape=jax.ShapeDtypeStruct((m, n), F32),
        compiler_params=_params(("parallel", "parallel")),
        name=name,
    )(x, w, w, res)


def _ssd_body(z_ref, x_ref, b_ref, c_ref, dt_ref,
              cwx_ref, cwb_ref, cwc_ref, cbx_ref, cbb_ref, cbc_ref,
              dtb_ref, alog_ref, dexp_ref, nw_ref,
              o_ref,
              tx_ref, tb_ref, tc_ref, st_ref, pad_ref):
    c = pl.program_id(1)
    g = pl.program_id(2)
    lc = x_ref.shape[1]

    @pl.when(c == 0)
    def _():
        tx_ref[g] = jnp.zeros(tx_ref.shape[1:], F32)
        tb_ref[g] = jnp.zeros(tb_ref.shape[1:], F32)
        tc_ref[g] = jnp.zeros(tc_ref.shape[1:], F32)
        st_ref[g] = jnp.zeros(st_ref.shape[1:], F32)

    def conv_silu(u_ref, w_ref, bias_ref, tail_ref):
        width = u_ref.shape[2]
        u = u_ref[0].astype(F32)
        pad_ref[0:SUBLANES, 0:width] = tail_ref[g]
        pad_ref[SUBLANES:SUBLANES + lc, 0:width] = u
        tail_ref[g] = u[lc - SUBLANES:lc, :]
        acc = bias_ref[...] + w_ref[SSD_CONV - 1:SSD_CONV, :] * u
        for k in range(SSD_CONV - 1):
            off = SUBLANES - (SSD_CONV - 1) + k
            acc = acc + w_ref[k:k + 1, :] * pad_ref[off:off + lc, 0:width]
        return acc * _sigmoid(acc)

    xg = conv_silu(x_ref, cwx_ref, cbx_ref, tx_ref)
    bg = conv_silu(b_ref, cwb_ref, cbb_ref, tb_ref)
    cg = conv_silu(c_ref, cwc_ref, cbc_ref, tc_ref)

    dt_in = dt_ref[0] + dtb_ref[0]
    dt = jnp.maximum(dt_in, 0.0) + jnp.log1p(jnp.exp(-jnp.abs(dt_in)))
    a = dt * (-jnp.exp(alog_ref[0]))
    rows = lax.broadcasted_iota(jnp.int32, (lc, lc), 0)
    cols = lax.broadcasted_iota(jnp.int32, (lc, lc), 1)
    tril = rows >= cols
    a_cum = jnp.dot(tril.astype(F32), a, precision=HIGHEST, preferred_element_type=F32)
    a_cum_t = a_cum.T
    dt_t = dt.T

    bgt = bg.T
    cb = jnp.dot(cg.astype(BF16), bgt.astype(BF16), preferred_element_type=F32)
    left = lax.broadcasted_iota(jnp.int32, (1, LANES), 1) < SSD_HEAD_DIM
    neg_inf = jnp.float32(-jnp.inf)

    ys = []
    for p in range(SSD_PAIRS):
        xp = xg[:, p * LANES:(p + 1) * LANES]
        xpb = xp.astype(BF16)
        st = st_ref[g, p]
        rhs = jnp.concatenate([xpb, st.astype(BF16)], axis=0)
        y_h, st_h = [], []
        for hh in range(2):
            r = 2 * p + hh
            colb = jnp.broadcast_to(a_cum[:, r:r + 1], (lc, LANES))
            rowb = a_cum_t[r:r + 1, :]
            dtr = dt_t[r:r + 1, :]
            a_tot = a_cum_t[r:r + 1, lc - 1:lc]
            seg = jnp.concatenate([colb] * (lc // LANES), axis=1) - rowb
            decay = jnp.exp(jnp.where(tril, seg, neg_inf))
            m = (cb * decay * dtr).astype(BF16)
            cs = (cg * jnp.exp(colb)).astype(BF16)
            lhs = jnp.concatenate([m, cs], axis=1)
            y_h.append(jnp.dot(lhs, rhs, preferred_element_type=F32))
            w_row = jnp.exp(a_tot - rowb) * dtr
            s_new = jnp.dot((bgt * w_row).astype(BF16), xpb, preferred_element_type=F32)
            st_h.append(st * jnp.exp(a_tot) + s_new)
        ys.append(jnp.where(left, y_h[0], y_h[1]) + dexp_ref[:, p * LANES:(p + 1) * LANES] * xp)
        st_ref[g, p] = jnp.where(left, st_h[0], st_h[1])

    y = jnp.concatenate(ys, axis=1)
    z = z_ref[0].astype(F32)
    yz = y * (z * _sigmoid(z))
    ms = jnp.mean(yz * yz, axis=-1, keepdims=True)
    o_ref[0] = (yz * lax.rsqrt(ms + RMS_EPS) * nw_ref[...]).astype(o_ref.dtype)


def _ssd(zx, dt, conv_w, conv_b, dt_bias, a_log, d_skip, norm_w):
    bsz, seq, _ = zx.shape
    g_n, gw, n = SSD_GROUPS, SSD_GROUP_W, SSD_STATE
    d_inner = g_n * gw
    lc = min(SSD_CHUNK, seq)
    nc = seq // lc
    xb0 = d_inner // gw
    bb0 = 2 * d_inner // n
    cb0 = bb0 + g_n
    cwx, cwb, cwc = conv_w[:, :d_inner], conv_w[:, d_inner:d_inner + g_n * n], conv_w[:, d_inner + g_n * n:]
    cb_ = conv_b.reshape(1, -1)
    cbx, cbb, cbc = cb_[:, :d_inner], cb_[:, d_inner:d_inner + g_n * n], cb_[:, d_inner + g_n * n:]
    pad_heads = lambda v: jnp.pad(v.reshape(g_n, 1, SSD_HEADS_PER_GROUP), ((0, 0), (0, 0), (0, LANES - SSD_HEADS_PER_GROUP)))
    dexp = jnp.repeat(d_skip, SSD_HEAD_DIM).reshape(1, d_inner)
    grp = lambda w: pl.BlockSpec((1, lc, w), lambda b, c, g: (b, c, g))
    return pl.pallas_call(
        _ssd_body,
        grid=(bsz, nc, g_n),
        in_specs=[
            pl.BlockSpec((1, lc, gw), lambda b, c, g: (b, c, g)),
            pl.BlockSpec((1, lc, gw), lambda b, c, g: (b, c, xb0 + g)),
            pl.BlockSpec((1, lc, n), lambda b, c, g: (b, c, bb0 + g)),
            pl.BlockSpec((1, lc, n), lambda b, c, g: (b, c, cb0 + g)),
            pl.BlockSpec((1, lc, LANES), lambda b, c, g: (b, c, g)),
            pl.BlockSpec((SSD_CONV, gw), lambda b, c, g: (0, g)),
            pl.BlockSpec((SSD_CONV, n), lambda b, c, g: (0, g)),
            pl.BlockSpec((SSD_CONV, n), lambda b, c, g: (0, g)),
            pl.BlockSpec((1, gw), lambda b, c, g: (0, g)),
            pl.BlockSpec((1, n), lambda b, c, g: (0, g)),
            pl.BlockSpec((1, n), lambda b, c, g: (0, g)),
            pl.BlockSpec((1, 1, LANES), lambda b, c, g: (g, 0, 0)),
            pl.BlockSpec((1, 1, LANES), lambda b, c, g: (g, 0, 0)),
            pl.BlockSpec((1, gw), lambda b, c, g: (0, g)),
            pl.BlockSpec((1, gw), lambda b, c, g: (0, g)),
        ],
        out_specs=pl.BlockSpec((1, lc, gw), lambda b, c, g: (b, c, g)),
        out_shape=jax.ShapeDtypeStruct((bsz, seq, d_inner), BF16),
        scratch_shapes=[
            pltpu.VMEM((g_n, SUBLANES, gw), F32),
            pltpu.VMEM((g_n, SUBLANES, n), F32),
            pltpu.VMEM((g_n, SUBLANES, n), F32),
            pltpu.VMEM((g_n, SSD_PAIRS, n, LANES), F32),
            pltpu.VMEM((SUBLANES + lc, gw), F32),
        ],
        compiler_params=_params(("arbitrary", "arbitrary", "arbitrary")),
        name="ssd_scan",
    )(zx, zx, zx, zx, dt, cwx, cwb, cwc, cbx, cbb, cbc, pad_heads(dt_bias), pad_heads(a_log), dexp,
      norm_w.reshape(1, d_inner))


def _ssd_mixer(h, xn, in_proj, conv_w, conv_b, dt_bias, a_log, d_skip, norm_w, out_proj, bsz, seq):
    t, d = h.shape
    n_heads = dt_bias.shape[0]
    d_main = in_proj.shape[1] - n_heads
    w_main = in_proj[:, :d_main].astype(BF16)
    w_dt = jnp.pad(in_proj[:, d_main:], ((0, 0), (0, LANES - n_heads))).astype(BF16)
    zx = _matmul(xn, w_main, BF16, 1024, 1024, "ssd_in_proj")
    dt_raw = _matmul(xn, w_dt, F32, 1024, LANES, "ssd_dt_proj")
    dt = dt_raw[:, :n_heads].reshape(t, SSD_GROUPS, SSD_HEADS_PER_GROUP)
    dt = jnp.pad(dt, ((0, 0), (0, 0), (0, LANES - SSD_HEADS_PER_GROUP))).reshape(bsz, seq, SSD_GROUPS * LANES)
    y = _ssd(zx.reshape(bsz, seq, d_main), dt, conv_w, conv_b, dt_bias, a_log, d_skip, norm_w)
    return _matmul_residual(y.reshape(t, -1), out_proj.astype(BF16), h, 512, 1024, "ssd_out_proj")


def _s5_body(u_ref, d_ref, win_ref, wout_ref, are_ref, aim_ref, o_ref, *, n_chunks):
    r = u_ref.shape[0] // S5_CHUNK
    pw = 2 * LANES
    n_pairs = S5_CHUNK // 2
    n_state = are_ref.shape[2]
    z = jnp.concatenate([u_ref[pl.ds(s, r, stride=S5_CHUNK), :] for s in range(S5_CHUNK)], axis=1).astype(BF16)
    xin = jnp.dot(z, win_ref[0], preferred_element_type=F32)
    xre, xim = xin[:, :n_state], xin[:, n_state:]
    kidx = lax.broadcasted_iota(jnp.int32, (r, n_state), 0) % n_chunks

    def shifted(v, sh):
        return jnp.where(kidx >= sh, pltpu.roll(v, sh, 0), 0.0)

    sh, lvl = 1, 0
    while sh < n_chunks:
        are, aim = are_ref[0, lvl:lvl + 1, :], aim_ref[0, lvl:lvl + 1, :]
        sre, sim = shifted(xre, sh), shifted(xim, sh)
        xre, xim = xre + are * sre - aim * sim, xim + are * sim + aim * sre
        sh, lvl = sh * 2, lvl + 1
    prev = jnp.concatenate([shifted(xre, 1), shifted(xim, 1)], axis=1).astype(BF16)
    carry = jnp.dot(prev, wout_ref[0], preferred_element_type=F32)
    for tp in range(n_pairs):
        y2 = jnp.dot(z[:, :(tp + 1) * pw], d_ref[0, (n_pairs - 1 - tp) * pw:, :], preferred_element_type=F32)
        y2 = y2 + carry[:, tp * pw:(tp + 1) * pw]
        gelu = 0.5 * y2 * (1.0 + jnp.tanh(math.sqrt(2.0 / math.pi) * (y2 + 0.044715 * (y2 * y2 * y2))))
        for i in range(2):
            o_ref[pl.ds(2 * tp + i, r, stride=S5_CHUNK), :] = gelu[:, i * LANES:(i + 1) * LANES]


def _s5_tables(lam_re, lam_im, log_step, b_re, b_im, c_re, c_im, d_skip, n_chunks):
    g_n, p_n = lam_re.shape
    ch, lcs, oct_ = S5_GROUP_CH, S5_CHUNK, S5_OCT
    n_oct = g_n // oct_
    lam = lax.complex(lam_re, lam_im)
    step = jnp.exp(log_step)[:, None]
    lam_bar = jnp.exp(lam * step)
    b_bar = ((lam_bar - 1.0) / lam)[..., None] * lax.complex(b_re, b_im)
    cc = lax.complex(c_re, c_im)
    taus = jnp.arange(lcs + 1, dtype=F32)
    lam_pow = jnp.exp((lam * step)[None] * taus[:, None, None])
    eye = jnp.eye(oct_, dtype=bool)
    kern = jnp.einsum('gcp,tgp,gpd->gtcd', cc, lam_pow[:lcs], b_bar, precision=HIGHEST).real
    kern = kern.at[:, 0].add(d_skip.reshape(g_n, ch)[:, :, None] * jnp.eye(ch, dtype=F32))
    kz = jnp.concatenate([jnp.zeros((g_n, 1, ch, ch), F32), kern], axis=1)
    dl = jnp.arange(lcs // 2)[:, None, None]
    s2 = jnp.arange(2)[None, :, None]
    t2 = jnp.arange(2)[None, None, :]
    blk = kz[:, 2 * dl + t2 - s2 + 1]
    blk = jnp.transpose(blk.reshape(n_oct, oct_, lcs // 2, 2, 2, ch, ch), (0, 2, 3, 1, 6, 4, 5))
    dtab = jnp.where(eye[None, None, None, :, None, None, :, None], blk[:, :, :, :, :, :, None, :], 0.0)
    dtab = dtab.reshape(n_oct, lcs // 2, 2 * LANES, 2 * LANES)[:, ::-1].reshape(n_oct, lcs * LANES, 2 * LANES)
    win = jnp.einsum('sgp,gpd->gsdp', lam_pow[:lcs][::-1], b_bar)
    win = jnp.transpose(win.reshape(n_oct, oct_, lcs, ch, p_n), (0, 2, 1, 3, 4))

    def embed_in(w):
        w = jnp.where(eye[None, None, :, None, :, None], w[:, :, :, :, None, :], 0.0)
        return w.reshape(n_oct, lcs * LANES, oct_ * p_n)

    win_t = jnp.concatenate([embed_in(win.real), embed_in(win.imag)], axis=2)
    wout = jnp.einsum('gcp,tgp->gptc', cc, lam_pow[1:lcs + 1]).reshape(n_oct, oct_, p_n, lcs, ch)

    def embed_out(w):
        w = jnp.where(eye[None, :, None, None, :, None], w[:, :, :, :, None, :], 0.0)
        return w.reshape(n_oct, oct_ * p_n, lcs * LANES)

    wout_t = jnp.concatenate([embed_out(wout.real), embed_out(-wout.imag)], axis=1)
    n_lvl = max(1, int(math.ceil(math.log2(max(n_chunks, 2)))))
    lvl_pow = lcs * (2.0 ** jnp.arange(n_lvl, dtype=F32))
    a_pow = jnp.exp((lam * step)[None] * lvl_pow[:, None, None])
    a_pow = jnp.transpose(a_pow.reshape(n_lvl, n_oct, oct_ * p_n), (1, 0, 2))
    a_pad = ((0, 0), (0, -n_lvl % SUBLANES), (0, 0))
    return (dtab.astype(BF16), win_t.astype(BF16), wout_t.astype(BF16),
            jnp.pad(a_pow.real, a_pad), jnp.pad(a_pow.imag, a_pad))


def _s5_mixer(h, xn, in_proj, lam_re, lam_im, log_step, b_re, b_im, c_re, c_im, d_skip, glu_proj, bsz, seq):
    t, d = h.shape
    g_n, p_n = lam_re.shape
    width = g_n * S5_GROUP_CH
    n_chunks = seq // S5_CHUNK
    n_oct = g_n // S5_OCT
    u = _matmul(xn, in_proj.astype(BF16), F32, 1024, 1024, "s5_in_proj")
    dtab, win_t, wout_t, are, aim = _s5_tables(lam_re, lam_im, log_step, b_re, b_im, c_re, c_im, d_skip, n_chunks)
    n_split = 2 if bsz % 2 == 0 else 1
    rows = t // n_split
    tab = lambda a: pl.BlockSpec((1,) + a.shape[1:], lambda j, i: (j, 0, 0))
    gact = pl.pallas_call(
        functools.partial(_s5_body, n_chunks=n_chunks),
        grid=(n_oct, n_split),
        in_specs=[pl.BlockSpec((rows, LANES), lambda j, i: (i, j)), tab(dtab), tab(win_t), tab(wout_t), tab(are), tab(aim)],
        out_specs=pl.BlockSpec((rows, LANES), lambda j, i: (i, j)),
        out_shape=jax.ShapeDtypeStruct((t, width), F32),
        compiler_params=_params(("parallel", "parallel")),
        name="s5_scan",
    )(u, dtab, win_t, wout_t, are, aim)
    return _matmul_glu_residual(gact, glu_proj.astype(BF16), h, 512, 1024, "s5_glu_proj")


def _router_body(h_ref, nw_ref, rw_ref, rb_ref, idx_ref, gate_ref, *, n_experts):
    xn = _rms(h_ref[...], nw_ref[...])
    logits = jnp.dot(xn, rw_ref[...], precision=HIGHEST, preferred_element_type=F32) + rb_ref[...]
    tm = logits.shape[0]
    lane = lax.broadcasted_iota(jnp.int32, (tm, LANES), 1)
    lane_f = lane.astype(F32)
    neg_inf = jnp.float32(-jnp.inf)
    cur = jnp.where(lane < n_experts, logits, neg_inf)
    tops, idxs = [], []
    for _ in range(TOP_K):
        mx = jnp.max(cur, axis=-1, keepdims=True)
        ix = jnp.min(jnp.where(cur == mx, lane_f, float(LANES)), axis=-1, keepdims=True)
        cur = jnp.where(lane_f == ix, neg_inf, cur)
        tops.append(mx)
        idxs.append(ix)
    es = [jnp.exp(v - tops[0]) for v in tops]
    denom = es[0] + es[1] + es[2] + es[3]
    idx_out = jnp.zeros((tm, LANES), F32)
    gate_out = jnp.zeros((tm, LANES), F32)
    for k in range(TOP_K):
        idx_out = jnp.where(lane == k, idxs[k], idx_out)
        gate_out = jnp.where(lane == k, es[k] / denom, gate_out)
    idx_ref[...] = idx_out.astype(jnp.int32)
    gate_ref[...] = gate_out


def _rank_body(idx_ref, rank_ref, cnt_ref, carry_ref):
    i = pl.program_id(0)
    tb = idx_ref.shape[0]

    @pl.when(i == 0)
    def _():
        carry_ref[...] = jnp.zeros(carry_ref.shape, F32)

    idx = idx_ref[...]
    lane = lax.broadcasted_iota(jnp.int32, (tb, LANES), 1)
    onehots = [(lane == idx[:, k:k + 1]) for k in range(TOP_K)]
    hits = jnp.zeros((tb, LANES), F32)
    for oh in onehots:
        hits = hits + oh.astype(F32)
    rows = lax.broadcasted_iota(jnp.int32, (tb, tb), 0)
    cols = lax.broadcasted_iota(jnp.int32, (tb, tb), 1)
    strict = (rows > cols).astype(BF16)
    before = jnp.dot(strict, hits.astype(BF16), preferred_element_type=F32) + carry_ref[0:1, :]
    out = jnp.zeros((tb, LANES), jnp.int32)
    for k, oh in enumerate(onehots):
        rk = jnp.sum(jnp.where(oh, before, 0.0), axis=-1, keepdims=True)
        out = jnp.where(lane == k, rk.astype(jnp.int32), out)
    rank_ref[...] = out
    total = carry_ref[0:1, :] + jnp.sum(hits, axis=0, keepdims=True)
    carry_ref[...] = jnp.broadcast_to(total, carry_ref.shape)
    cnt_ref[...] = jnp.broadcast_to(total, cnt_ref.shape)


def _dispatch_body(dest_ref, h_ref, nw_ref, xs_in_ref, xs_ref, buf_ref, sem):
    del xs_in_ref
    tt = h_ref.shape[0]
    buf_ref[...] = _rms(h_ref[...], nw_ref[...])

    def row_copy(i, k):
        d = dest_ref[0, 0, i * TOP_K + k]
        return pltpu.make_async_copy(buf_ref.at[pl.ds(i, 1), :], xs_ref.at[pl.ds(d, 1), :], sem)

    def start(i, carry):
        for k in range(TOP_K):
            row_copy(i, k).start()
        return carry

    def wait(i, carry):
        for k in range(TOP_K):
            row_copy(i, k).wait()
        return carry

    lax.fori_loop(0, tt, start, 0)
    lax.fori_loop(0, tt, wait, 0)


def _combine_body(dest_ref, h_ref, gate_ref, ys_ref, o_ref, buf_ref, sem):
    tt = h_ref.shape[0]

    def row_copy(i, k):
        d = dest_ref[0, 0, i * TOP_K + k]
        return pltpu.make_async_copy(ys_ref.at[pl.ds(d, 1), :], buf_ref.at[k, pl.ds(i, 1), :], sem)

    def start(i, carry):
        for k in range(TOP_K):
            row_copy(i, k).start()
        return carry

    def wait(i, carry):
        for k in range(TOP_K):
            row_copy(i, k).wait()
        return carry

    lax.fori_loop(0, tt, start, 0)
    lax.fori_loop(0, tt, wait, 0)
    gates = gate_ref[...]
    acc = h_ref[...]
    for k in range(TOP_K):
        acc = acc + gates[:, k:k + 1] * buf_ref[k]
    o_ref[...] = acc


def _expert_body(be_ref, ns_ref, nu_ref, x_ref, wg_ref, wu_ref, wd_ref, bg_ref, bu_ref, bd_ref, o_ref):
    del be_ref, nu_ref
    b = pl.program_id(0)
    f = pl.program_id(1)
    n_sub = ns_ref[b]
    d = o_ref.shape[1]

    @pl.when(f == 0)
    def _():
        o_ref[...] = jnp.broadcast_to(bd_ref[0, 0], o_ref.shape)

    for ns in range(1, MOE_TM // MOE_SUB + 1):
        @pl.when(n_sub == ns)
        def _():
            rows = ns * MOE_SUB
            x = x_ref[0:rows, :].astype(BF16)
            gate = jnp.dot(x, wg_ref[0, 0].astype(BF16), preferred_element_type=F32) + bg_ref[0, 0]
            up = jnp.dot(x, wu_ref[0, 0].astype(BF16), preferred_element_type=F32) + bu_ref[0, 0]
            gate = jnp.minimum(gate, SWIGLU_LIMIT)
            up = jnp.clip(up, -SWIGLU_LIMIT, SWIGLU_LIMIT)
            act = ((up + 1.0) * (gate * _sigmoid(SWIGLU_ALPHA * gate))).astype(BF16)
            for half in range(2):
                cs = slice(half * d // 2, (half + 1) * d // 2)
                wd = wd_ref[0, 0, :, cs].astype(BF16)
                o_ref[0:rows, cs] = o_ref[0:rows, cs] + jnp.dot(act, wd, preferred_element_type=F32)


def _moe(h, norm_w, router_w, router_b, w_gate_up, b_gate_up, w_down, b_down, layer):
    t, d = h.shape
    n_e = router_w.shape[1]
    d_ff = w_down.shape[2]
    nw = norm_w.reshape(1, d)
    tr = min(512, t)
    rw = jnp.pad(router_w, ((0, 0), (0, LANES - n_e)))
    rb = jnp.pad(router_b, (0, LANES - n_e)).reshape(1, LANES)
    idx, gates = pl.pallas_call(
        functools.partial(_router_body, n_experts=n_e),
        grid=(t // tr,),
        in_specs=[pl.BlockSpec((tr, d), lambda i: (i, 0)), pl.BlockSpec((1, d), lambda i: (0, 0)),
                  pl.BlockSpec((d, LANES), lambda i: (0, 0)), pl.BlockSpec((1, LANES), lambda i: (0, 0))],
        out_specs=[pl.BlockSpec((tr, LANES), lambda i: (i, 0)), pl.BlockSpec((tr, LANES), lambda i: (i, 0))],
        out_shape=[jax.ShapeDtypeStruct((t, LANES), jnp.int32), jax.ShapeDtypeStruct((t, LANES), F32)],
        compiler_params=_params(("parallel",)),
        name="moe_router",
    )(h, nw, rw, rb)
    tb = min(512, t)
    rank, cnt = pl.pallas_call(
        _rank_body,
        grid=(t // tb,),
        in_specs=[pl.BlockSpec((tb, LANES), lambda i: (i, 0))],
        out_specs=[pl.BlockSpec((tb, LANES), lambda i: (i, 0)), pl.BlockSpec((SUBLANES, LANES), lambda i: (0, 0))],
        out_shape=[jax.ShapeDtypeStruct((t, LANES), jnp.int32), jax.ShapeDtypeStruct((SUBLANES, LANES), F32)],
        scratch_shapes=[pltpu.VMEM((SUBLANES, LANES), F32)],
        compiler_params=_params(("arbitrary",)),
        name="moe_rank",
    )(idx)
    counts = cnt[0, :n_e].astype(jnp.int32)
    n_blocks = (t * TOP_K) // MOE_TM + n_e
    blocks_e = (counts + MOE_TM - 1) // MOE_TM
    blk_end = jnp.cumsum(blocks_e)
    blk_start = blk_end - blocks_e
    n_used = blk_end[-1]
    bid = jnp.arange(n_blocks, dtype=jnp.int32)
    blk_expert = jnp.minimum(jnp.searchsorted(blk_end, bid, side='right'), n_e - 1).astype(jnp.int32)
    last_used_expert = blk_expert[jnp.maximum(n_used - 1, 0)]
    blk_expert = jnp.where(bid < n_used, blk_expert, last_used_expert)
    rows_in_blk = jnp.clip(counts[blk_expert] - (bid - blk_start[blk_expert]) * MOE_TM, 0, MOE_TM)
    n_sub = jnp.where(bid < n_used, (rows_in_blk + MOE_SUB - 1) // MOE_SUB, 0).astype(jnp.int32)
    dest = blk_start[idx[:, :TOP_K]] * MOE_TM + rank[:, :TOP_K]
    tt = min(256, t)
    dest3 = dest.reshape(t // tt, 1, tt * TOP_K)
    n_rows = n_blocks * MOE_TM
    xs = pl.pallas_call(
        _dispatch_body,
        grid=(t // tt,),
        in_specs=[pl.BlockSpec((1, 1, tt * TOP_K), lambda i: (i, 0, 0), memory_space=pltpu.SMEM),
                  pl.BlockSpec((tt, d), lambda i: (i, 0)), pl.BlockSpec((1, d), lambda i: (0, 0)),
                  pl.BlockSpec(memory_space=pl.ANY)],
        out_specs=pl.BlockSpec(memory_space=pl.ANY),
        out_shape=jax.ShapeDtypeStruct((n_rows, d), F32),
        scratch_shapes=[pltpu.VMEM((tt, d), F32), pltpu.SemaphoreType.DMA],
        input_output_aliases={3: 0},
        compiler_params=_params(("arbitrary",)),
        name="moe_dispatch",
    )(dest3, h, nw, jnp.zeros((n_rows, d), F32))
    nf = d_ff // MOE_TF
    last_f = nf - 1
    used = lambda b, f, nu: jnp.where(b < nu[0], f, last_f)
    ys = pl.pallas_call(
        _expert_body,
        grid_spec=pltpu.PrefetchScalarGridSpec(
            num_scalar_prefetch=3,
            grid=(n_blocks, nf),
            in_specs=[
                pl.BlockSpec((MOE_TM, d), lambda b, f, be, ns, nu: (jnp.minimum(b, jnp.maximum(nu[0] - 1, 0)), 0)),
                pl.BlockSpec((1, 1, d, MOE_TF), lambda b, f, be, ns, nu: (layer, be[b], 0, used(b, f, nu))),
                pl.BlockSpec((1, 1, d, MOE_TF), lambda b, f, be, ns, nu: (layer, be[b], 0, nf + used(b, f, nu))),
                pl.BlockSpec((1, 1, MOE_TF, d), lambda b, f, be, ns, nu: (layer, be[b], used(b, f, nu), 0)),
                pl.BlockSpec((1, 1, 1, MOE_TF), lambda b, f, be, ns, nu: (layer, be[b], 0, used(b, f, nu))),
                pl.BlockSpec((1, 1, 1, MOE_TF), lambda b, f, be, ns, nu: (layer, be[b], 0, nf + used(b, f, nu))),
                pl.BlockSpec((1, 1, 1, d), lambda b, f, be, ns, nu: (layer, be[b], 0, 0)),
            ],
            out_specs=pl.BlockSpec((MOE_TM, d), lambda b, f, be, ns, nu: (b, 0)),
        ),
        out_shape=jax.ShapeDtypeStruct((n_rows, d), F32),
        compiler_params=_params(("arbitrary", "arbitrary")),
        name="moe_experts",
    )(blk_expert, n_sub, n_used.reshape(1).astype(jnp.int32), xs, w_gate_up, w_gate_up, w_down,
      b_gate_up.reshape(-1, n_e, 1, 2 * d_ff), b_gate_up.reshape(-1, n_e, 1, 2 * d_ff), b_down.reshape(-1, n_e, 1, d))
    tc = min(128, t)
    destc = dest.reshape(t // tc, 1, tc * TOP_K)
    return pl.pallas_call(
        _combine_body,
        grid=(t // tc,),
        in_specs=[pl.BlockSpec((1, 1, tc * TOP_K), lambda i: (i, 0, 0), memory_space=pltpu.SMEM),
                  pl.BlockSpec((tc, d), lambda i: (i, 0)), pl.BlockSpec((tc, LANES), lambda i: (i, 0)),
                  pl.BlockSpec(memory_space=pl.ANY)],
        out_specs=pl.BlockSpec((tc, d), lambda i: (i, 0)),
        out_shape=jax.ShapeDtypeStruct((t, d), F32),
        scratch_shapes=[pltpu.VMEM((TOP_K, tc, d), F32), pltpu.SemaphoreType.DMA],
        compiler_params=_params(("arbitrary",)),
        name="moe_combine",
    )(destc, h, gates, ys)


def kernel(x, norm_mix_w, norm_ffn_w, final_norm_w, a_in_proj, a_conv_w, a_conv_b, a_dt_bias, a_log, a_d_skip, a_norm_w, a_out_proj, b_in_proj, b_lam_re, b_lam_im, b_log_step, b_b_re, b_b_im, b_c_re, b_c_im, b_d_skip, b_glu_proj, router_w, router_b, w_gate_up, b_gate_up, w_down, b_down):
    bsz, seq, d = x.shape
    depth = norm_mix_w.shape[0]
    h = x.reshape(bsz * seq, d)
    for i in range(depth):
        xn = _rmsnorm(h, norm_mix_w[i], BF16)
        j = i // 2
        if i % 2 == 0:
            h = _ssd_mixer(h, xn, a_in_proj[j], a_conv_w[j], a_conv_b[j], a_dt_bias[j], a_log[j], a_d_skip[j],
                           a_norm_w[j], a_out_proj[j], bsz, seq)
        else:
            h = _s5_mixer(h, xn, b_in_proj[j], b_lam_re[j], b_lam_im[j], b_log_step[j], b_b_re[j], b_b_im[j],
                          b_c_re[j], b_c_im[j], b_d_skip[j], b_glu_proj[j], bsz, seq)
        h = _moe(h, norm_ffn_w[i], router_w[i], router_b[i], w_gate_up, b_gate_up, w_down, b_down, i)
    return _rmsnorm(h, final_norm_w, F32).reshape(bsz, seq, d)
```

```python
import functools
import math

import jax
import jax.numpy as jnp
from jax import lax
from jax.experimental import pallas as pl
from jax.experimental.pallas import tpu as pltpu

F32 = jnp.float32
BF16 = jnp.bfloat16
HIGHEST = lax.Precision.HIGHEST

RMS_EPS = 1e-6
LOG2_E = 1.0 / math.log(2.0)
LANES = 128
SUBLANES = 8
V7X_VMEM_LIMIT_BYTES = 58 * 1024 * 1024

SSD_HEAD_DIM = 64
SSD_GROUPS = 8
SSD_HEADS_PER_GROUP = 8
SSD_STATE = 128
SSD_CONV = 4
SSD_CHUNK = 256
SSD_GROUP_W = SSD_HEADS_PER_GROUP * SSD_HEAD_DIM
SSD_PAIRS = SSD_GROUP_W // LANES

S5_GROUP_CH = 16
S5_STATE = 64
S5_CHUNK = 16
S5_OCT = LANES // S5_GROUP_CH

TOP_K = 4
SWIGLU_LIMIT = 7.0
SWIGLU_ALPHA = 1.702
MOE_TM = 1024
MOE_SUB = 256
MOE_TF = 256


def _params(semantics):
    return pltpu.CompilerParams(dimension_semantics=semantics, vmem_limit_bytes=V7X_VMEM_LIMIT_BYTES)


def _sigmoid(x):
    return 0.5 + 0.5 * jnp.tanh(0.5 * x)


def _silu(x):
    hx = 0.5 * x
    return hx + hx * jnp.tanh(hx)


def _pack_bf16_pair(lo, hi):
    ulo = lax.bitcast_convert_type(lo.astype(BF16).astype(F32), jnp.uint32)
    uhi = lax.bitcast_convert_type(hi.astype(BF16).astype(F32), jnp.uint32)
    return (ulo >> 16) | uhi


def _unpack_bf16_pair(w):
    lo = lax.bitcast_convert_type(w << 16, F32)
    hi = lax.bitcast_convert_type(w & jnp.uint32(0xFFFF0000), F32)
    return lo, hi


def _rms(x, w):
    ms = jnp.mean(x * x, axis=-1, keepdims=True)
    return x * lax.rsqrt(ms + RMS_EPS) * w


def _rmsnorm_body(x_ref, w_ref, o_ref):
    o_ref[...] = _rms(x_ref[...], w_ref[...]).astype(o_ref.dtype)


def _rmsnorm(x, w, out_dtype, tm=512):
    t, d = x.shape
    tm = min(tm, t)
    return pl.pallas_call(
        _rmsnorm_body,
        grid=(t // tm,),
        in_specs=[pl.BlockSpec((tm, d), lambda i: (i, 0)), pl.BlockSpec((1, d), lambda i: (0, 0))],
        out_specs=pl.BlockSpec((tm, d), lambda i: (i, 0)),
        out_shape=jax.ShapeDtypeStruct((t, d), out_dtype),
        compiler_params=_params(("parallel",)),
        name="rmsnorm",
    )(x, w.reshape(1, d))


def _mm_body(x_ref, w_ref, o_ref):
    o_ref[...] = jnp.dot(x_ref[...], w_ref[...], preferred_element_type=F32).astype(o_ref.dtype)


def _mm_res_body(x_ref, w_ref, r_ref, o_ref):
    o_ref[...] = r_ref[...] + jnp.dot(x_ref[...], w_ref[...], preferred_element_type=F32)


def _mm_glu_body(x_ref, wv_ref, wg_ref, r_ref, o_ref):
    x = x_ref[...].astype(BF16)
    v = jnp.dot(x, wv_ref[...], preferred_element_type=F32)
    g = jnp.dot(x, wg_ref[...], preferred_element_type=F32)
    o_ref[...] = r_ref[...] + v * _sigmoid(g)


def _matmul(x, w, out_dtype, tm, tn, name):
    m, k = x.shape
    n = w.shape[1]
    tm, tn = min(tm, m), min(tn, n)
    return pl.pallas_call(
        _mm_body,
        grid=(m // tm, n // tn),
        in_specs=[pl.BlockSpec((tm, k), lambda i, j: (i, 0)), pl.BlockSpec((k, tn), lambda i, j: (0, j))],
        out_specs=pl.BlockSpec((tm, tn), lambda i, j: (i, j)),
        out_shape=jax.ShapeDtypeStruct((m, n), out_dtype),
        compiler_params=_params(("parallel", "parallel")),
        name=name,
    )(x, w)


def _matmul_residual(x, w, res, tm, tn, name):
    m, k = x.shape
    n = w.shape[1]
    tm, tn = min(tm, m), min(tn, n)
    return pl.pallas_call(
        _mm_res_body,
        grid=(m // tm, n // tn),
        in_specs=[pl.BlockSpec((tm, k), lambda i, j: (i, 0)), pl.BlockSpec((k, tn), lambda i, j: (0, j)),
                  pl.BlockSpec((tm, tn), lambda i, j: (i, j))],
        out_specs=pl.BlockSpec((tm, tn), lambda i, j: (i, j)),
        out_shape=jax.ShapeDtypeStruct((m, n), F32),
        compiler_params=_params(("parallel", "parallel")),
        name=name,
    )(x, w, res)


def _matmul_glu_residual(x, w, res, tm, tn, name):
    m, k = x.shape
    n = w.shape[1] // 2
    tm, tn = min(tm, m), min(tn, n)
    nj = n // tn
    return pl.pallas_call(
        _mm_glu_body,
        grid=(m // tm, nj),
        in_specs=[pl.BlockSpec((tm, k), lambda i, j: (i, 0)),
                  pl.BlockSpec((k, tn), lambda i, j: (0, j)),
                  pl.BlockSpec((k, tn), lambda i, j: (0, nj + j)),
                  pl.BlockSpec((tm, tn), lambda i, j: (i, j))],
        out_specs=pl.BlockSpec((tm, tn), lambda i, j: (i, j)),
        out_shape=jax.ShapeDtypeStruct((m, n), F32),
        compiler_params=_params(("parallel", "parallel")),
        name=name,
    )(x, w, w, res)


def _ssd_body(z_ref, x_ref, b_ref, c_ref, dt_ref,
              cwx_ref, cwb_ref, cwc_ref, cbx_ref, cbb_ref, cbc_ref,
              dtb_ref, alog_ref, dexp_ref, nw_ref,
              o_ref,
              tx_ref, tb_ref, tc_ref, st_ref, pad_ref):
    c = pl.program_id(1)
    g = pl.program_id(2)
    lc = x_ref.shape[1]

    @pl.when(c == 0)
    def _():
        tx_ref[g] = jnp.zeros(tx_ref.shape[1:], F32)
        tb_ref[g] = jnp.zeros(tb_ref.shape[1:], F32)
        tc_ref[g] = jnp.zeros(tc_ref.shape[1:], F32)
        st_ref[g] = jnp.zeros(st_ref.shape[1:], F32)

    def conv_silu(u_ref, w_ref, bias_ref, tail_ref):
        width = u_ref.shape[2]
        u = u_ref[0].astype(F32)
        pad_ref[0:SUBLANES, 0:width] = tail_ref[g]
        pad_ref[SUBLANES:2 * SUBLANES, 0:width] = u[0:SUBLANES, :]
        tail_ref[g] = u[lc - SUBLANES:lc, :]
        acc = bias_ref[...] + w_ref[SSD_CONV - 1:SSD_CONV, :] * u
        for k in range(SSD_CONV - 1):
            sh = SSD_CONV - 1 - k
            body = pltpu.roll(u, sh, 0)
            head = pad_ref[SUBLANES - sh:2 * SUBLANES - sh, 0:width]
            acc = acc + w_ref[k:k + 1, :] * jnp.concatenate([head, body[SUBLANES:, :]], axis=0)
        return _silu(acc)

    xg = conv_silu(x_ref, cwx_ref, cbx_ref, tx_ref)
    bg = conv_silu(b_ref, cwb_ref, cbb_ref, tb_ref)
    cg = conv_silu(c_ref, cwc_ref, cbc_ref, tc_ref)

    dt_in = dt_ref[0] + dtb_ref[0]
    dt = jnp.maximum(dt_in, 0.0) + jnp.log1p(jnp.exp(-jnp.abs(dt_in)))
    a = dt * (-jnp.exp(alog_ref[0]))
    rows = lax.broadcasted_iota(jnp.int32, (lc, lc), 0)
    cols = lax.broadcasted_iota(jnp.int32, (lc, lc), 1)
    a_cum = jnp.dot((rows >= cols).astype(F32), a, precision=HIGHEST, preferred_element_type=F32)
    a2 = a_cum * LOG2_E
    a2_t = a2.T
    src_t = a2_t - jnp.log(dt.T) * LOG2_E

    bgt = bg.T
    cb = jnp.dot(cg.astype(BF16), bgt.astype(BF16), preferred_element_type=F32)
    left = lax.broadcasted_iota(jnp.int32, (1, LANES), 1) < SSD_HEAD_DIM
    n_blk = lc // LANES
    diag_mask = (lax.broadcasted_iota(jnp.int32, (LANES, LANES), 0) >= lax.broadcasted_iota(jnp.int32, (LANES, LANES), 1))
    neg_inf = jnp.float32(-jnp.inf)

    ys = []
    for p in range(SSD_PAIRS):
        xp = xg[:, p * LANES:(p + 1) * LANES]
        xpb = xp.astype(BF16)
        st = st_ref[g, p]
        stb = st.astype(BF16)
        y_h, st_h = [], []
        for hh in range(2):
            r = 2 * p + hh
            colb = jnp.broadcast_to(a2[:, r:r + 1], (lc, LANES))
            srow = src_t[r:r + 1, :]
            a2_tot = a2_t[r:r + 1, lc - 1:lc]
            cs = (cg * jnp.exp2(colb)).astype(BF16)
            y_rows = []
            for i in range(n_blk):
                ci = colb[i * LANES:(i + 1) * LANES]
                parts = []
                for j in range(i + 1):
                    seg = ci - srow[:, j * LANES:(j + 1) * LANES]
                    if j == i:
                        seg = jnp.where(diag_mask, seg, neg_inf)
                    parts.append((cb[i * LANES:(i + 1) * LANES, j * LANES:(j + 1) * LANES] * jnp.exp2(seg)).astype(BF16))
                lhs = jnp.concatenate(parts + [cs[i * LANES:(i + 1) * LANES]], axis=1)
                rhs = jnp.concatenate([xpb[:(i + 1) * LANES], stb], axis=0)
                y_rows.append(jnp.dot(lhs, rhs, preferred_element_type=F32))
            y_h.append(jnp.concatenate(y_rows, axis=0))
            w_row = jnp.exp2(a2_tot - srow)
            s_new = jnp.dot((bgt * w_row).astype(BF16), xpb, preferred_element_type=F32)
            st_h.append(st * jnp.exp2(a2_tot) + s_new)
        ys.append(jnp.where(left, y_h[0], y_h[1]) + dexp_ref[:, p * LANES:(p + 1) * LANES] * xp)
        st_ref[g, p] = jnp.where(left, st_h[0], st_h[1])

    y = jnp.concatenate(ys, axis=1)
    yz = y * _silu(z_ref[0].astype(F32))
    ms = jnp.mean(yz * yz, axis=-1, keepdims=True)
    o_ref[0] = (yz * lax.rsqrt(ms + RMS_EPS) * nw_ref[...]).astype(o_ref.dtype)


def _ssd(zx, dt, conv_w, conv_b, dt_bias, a_log, d_skip, norm_w):
    bsz, seq, _ = zx.shape
    g_n, gw, n = SSD_GROUPS, SSD_GROUP_W, SSD_STATE
    d_inner = g_n * gw
    lc = min(SSD_CHUNK, seq)
    nc = seq // lc
    xb0 = d_inner // gw
    bb0 = 2 * d_inner // n
    cb0 = bb0 + g_n
    cwx, cwb, cwc = conv_w[:, :d_inner], conv_w[:, d_inner:d_inner + g_n * n], conv_w[:, d_inner + g_n * n:]
    cb_ = conv_b.reshape(1, -1)
    cbx, cbb, cbc = cb_[:, :d_inner], cb_[:, d_inner:d_inner + g_n * n], cb_[:, d_inner + g_n * n:]
    pad_heads = lambda v: jnp.pad(v.reshape(g_n, 1, SSD_HEADS_PER_GROUP), ((0, 0), (0, 0), (0, LANES - SSD_HEADS_PER_GROUP)))
    dexp = jnp.repeat(d_skip, SSD_HEAD_DIM).reshape(1, d_inner)
    grp = lambda w: pl.BlockSpec((1, lc, w), lambda b, c, g: (b, c, g))
    return pl.pallas_call(
        _ssd_body,
        grid=(bsz, nc, g_n),
        in_specs=[
            pl.BlockSpec((1, lc, gw), lambda b, c, g: (b, c, g)),
            pl.BlockSpec((1, lc, gw), lambda b, c, g: (b, c, xb0 + g)),
            pl.BlockSpec((1, lc, n), lambda b, c, g: (b, c, bb0 + g)),
            pl.BlockSpec((1, lc, n), lambda b, c, g: (b, c, cb0 + g)),
            pl.BlockSpec((1, lc, LANES), lambda b, c, g: (b, c, g)),
            pl.BlockSpec((SSD_CONV, gw), lambda b, c, g: (0, g)),
            pl.BlockSpec((SSD_CONV, n), lambda b, c, g: (0, g)),
            pl.BlockSpec((SSD_CONV, n), lambda b, c, g: (0, g)),
            pl.BlockSpec((1, gw), lambda b, c, g: (0, g)),
            pl.BlockSpec((1, n), lambda b, c, g: (0, g)),
            pl.BlockSpec((1, n), lambda b, c, g: (0, g)),
            pl.BlockSpec((1, 1, LANES), lambda b, c, g: (g, 0, 0)),
            pl.BlockSpec((1, 1, LANES), lambda b, c, g: (g, 0, 0)),
            pl.BlockSpec((1, gw), lambda b, c, g: (0, g)),
            pl.BlockSpec((1, gw), lambda b, c, g: (0, g)),
        ],
        out_specs=pl.BlockSpec((1, lc, gw), lambda b, c, g: (b, c, g)),
        out_shape=jax.ShapeDtypeStruct((bsz, seq, d_inner), BF16),
        scratch_shapes=[
            pltpu.VMEM((g_n, SUBLANES, gw), F32),
            pltpu.VMEM((g_n, SUBLANES, n), F32),
            pltpu.VMEM((g_n, SUBLANES, n), F32),
            pltpu.VMEM((g_n, SSD_PAIRS, n, LANES), F32),
            pltpu.VMEM((2 * SUBLANES, gw), F32),
        ],
        compiler_params=_params(("arbitrary", "arbitrary", "arbitrary")),
        name="ssd_scan",
    )(zx, zx, zx, zx, dt, cwx, cwb, cwc, cbx, cbb, cbc, pad_heads(dt_bias), pad_heads(a_log), dexp,
      norm_w.reshape(1, d_inner))


def _ssd_mixer(h, xn, in_proj, conv_w, conv_b, dt_bias, a_log, d_skip, norm_w, out_proj, bsz, seq):
    t, d = h.shape
    n_heads = dt_bias.shape[0]
    d_main = in_proj.shape[1] - n_heads
    w_main = in_proj[:, :d_main].astype(BF16)
    w_dt = jnp.pad(in_proj[:, d_main:], ((0, 0), (0, LANES - n_heads))).astype(BF16)
    zx = _matmul(xn, w_main, BF16, 1024, 1024, "ssd_in_proj")
    dt_raw = _matmul(xn, w_dt, F32, 1024, LANES, "ssd_dt_proj")
    dt = dt_raw[:, :n_heads].reshape(t, SSD_GROUPS, SSD_HEADS_PER_GROUP)
    dt = jnp.pad(dt, ((0, 0), (0, 0), (0, LANES - SSD_HEADS_PER_GROUP))).reshape(bsz, seq, SSD_GROUPS * LANES)
    y = _ssd(zx.reshape(bsz, seq, d_main), dt, conv_w, conv_b, dt_bias, a_log, d_skip, norm_w)
    return _matmul_residual(y.reshape(t, -1), out_proj.astype(BF16), h, 512, 1024, "ssd_out_proj")


def _s5_body(u_ref, d_ref, win_ref, wout_ref, are_ref, aim_ref, o_ref, *, n_chunks):
    r = u_ref.shape[0] // S5_CHUNK
    pw = 2 * LANES
    n_pairs = S5_CHUNK // 2
    n_state = are_ref.shape[2]
    z = jnp.concatenate([u_ref[pl.ds(s, r, stride=S5_CHUNK), :] for s in range(S5_CHUNK)], axis=1).astype(BF16)
    xin = jnp.dot(z, win_ref[0], preferred_element_type=F32)
    xre, xim = xin[:, :n_state], xin[:, n_state:]
    kidx = lax.broadcasted_iota(jnp.int32, (r, n_state), 0) % n_chunks

    def shifted(v, sh):
        return jnp.where(kidx >= sh, pltpu.roll(v, sh, 0), 0.0)

    sh, lvl = 1, 0
    while sh < n_chunks:
        are, aim = are_ref[0, lvl:lvl + 1, :], aim_ref[0, lvl:lvl + 1, :]
        sre, sim = shifted(xre, sh), shifted(xim, sh)
        xre, xim = xre + are * sre - aim * sim, xim + are * sim + aim * sre
        sh, lvl = sh * 2, lvl + 1
    prev = jnp.concatenate([shifted(xre, 1), shifted(xim, 1)], axis=1).astype(BF16)
    carry = jnp.dot(prev, wout_ref[0], preferred_element_type=F32)
    for tp in range(n_pairs):
        y2 = jnp.dot(z[:, :(tp + 1) * pw], d_ref[0, (n_pairs - 1 - tp) * pw:, :], preferred_element_type=F32)
        y2 = y2 + carry[:, tp * pw:(tp + 1) * pw]
        gelu = 0.5 * y2 * (1.0 + jnp.tanh(math.sqrt(2.0 / math.pi) * (y2 + 0.044715 * (y2 * y2 * y2))))
        for i in range(2):
            o_ref[pl.ds(2 * tp + i, r, stride=S5_CHUNK), :] = gelu[:, i * LANES:(i + 1) * LANES]


def _s5_tables(lam_re, lam_im, log_step, b_re, b_im, c_re, c_im, d_skip, n_chunks):
    g_n, p_n = lam_re.shape
    ch, lcs, oct_ = S5_GROUP_CH, S5_CHUNK, S5_OCT
    n_oct = g_n // oct_
    lam = lax.complex(lam_re, lam_im)
    step = jnp.exp(log_step)[:, None]
    lam_bar = jnp.exp(lam * step)
    b_bar = ((lam_bar - 1.0) / lam)[..., None] * lax.complex(b_re, b_im)
    cc = lax.complex(c_re, c_im)
    taus = jnp.arange(lcs + 1, dtype=F32)
    lam_pow = jnp.exp((lam * step)[None] * taus[:, None, None])

    def spread(w, src_lane, grp_lane, g_axis):
        sel = (jnp.arange(w.shape[-1])[:, None] == src_lane[None, :]).astype(BF16)
        v = jnp.dot(w.astype(BF16), sel, preferred_element_type=F32)
        gshape = [1] * v.ndim
        gshape[g_axis] = oct_
        return jnp.where(jnp.arange(oct_).reshape(gshape) == grp_lane, v, 0.0).astype(BF16)

    kern = jnp.einsum('gcp,tgp,gpd->gtcd', cc, lam_pow[:lcs], b_bar, precision=HIGHEST).real
    kern = kern.at[:, 0].add(d_skip.reshape(g_n, ch)[:, :, None] * jnp.eye(ch, dtype=F32))
    kz = jnp.concatenate([jnp.zeros((g_n, 1, ch, ch), F32), kern], axis=1)
    dl = jnp.arange(lcs // 2)[:, None, None]
    s2 = jnp.arange(2)[None, :, None]
    t2 = jnp.arange(2)[None, None, :]
    blk = kz[:, 2 * dl + t2 - s2 + 1]
    blk = jnp.transpose(blk.reshape(n_oct, oct_, lcs // 2, 2, 2, ch, ch), (0, 2, 3, 1, 6, 4, 5))
    lane2 = jnp.arange(2 * LANES)
    dtab = spread(blk.reshape(n_oct, lcs // 2, 2, oct_, ch, 2 * ch),
                  (lane2 // LANES) * ch + lane2 % ch, (lane2 % LANES) // ch, 3)
    dtab = dtab.reshape(n_oct, lcs // 2, 2 * LANES, 2 * LANES)[:, ::-1].reshape(n_oct, lcs * LANES, 2 * LANES)
    win = jnp.einsum('sgp,gpd->gsdp', lam_pow[:lcs][::-1], b_bar)
    win = jnp.transpose(win.reshape(n_oct, oct_, lcs, ch, p_n), (0, 2, 1, 3, 4))
    n_state = oct_ * p_n
    lane_s = jnp.arange(2 * n_state)
    win_t = spread(jnp.concatenate([win.real, win.imag], axis=-1),
                   (lane_s // n_state) * p_n + lane_s % p_n, (lane_s % n_state) // p_n, 2)
    win_t = win_t.reshape(n_oct, lcs * LANES, 2 * n_state)
    wout = jnp.einsum('gcp,tgp->gptc', cc, lam_pow[1:lcs + 1]).reshape(n_oct, oct_, p_n, lcs * ch)
    lane_o = jnp.arange(lcs * LANES)
    src_o, grp_o = (lane_o // LANES) * ch + lane_o % ch, (lane_o % LANES) // ch
    wout_t = jnp.concatenate([spread(wout.real, src_o, grp_o, 1).reshape(n_oct, n_state, lcs * LANES),
                              spread(-wout.imag, src_o, grp_o, 1).reshape(n_oct, n_state, lcs * LANES)], axis=1)
    n_lvl = max(1, int(math.ceil(math.log2(max(n_chunks, 2)))))
    lvl_pow = lcs * (2.0 ** jnp.arange(n_lvl, dtype=F32))
    a_pow = jnp.exp((lam * step)[None] * lvl_pow[:, None, None])
    a_pow = jnp.transpose(a_pow.reshape(n_lvl, n_oct, oct_ * p_n), (1, 0, 2))
    a_pad = ((0, 0), (0, -n_lvl % SUBLANES), (0, 0))
    return (dtab, win_t, wout_t,
            jnp.pad(a_pow.real, a_pad), jnp.pad(a_pow.imag, a_pad))


def _s5_mixer(h, xn, in_proj, lam_re, lam_im, log_step, b_re, b_im, c_re, c_im, d_skip, glu_proj, bsz, seq):
    t, d = h.shape
    g_n, p_n = lam_re.shape
    width = g_n * S5_GROUP_CH
    n_chunks = seq // S5_CHUNK
    n_oct = g_n // S5_OCT
    u = _matmul(xn, in_proj.astype(BF16), F32, 1024, 1024, "s5_in_proj")
    dtab, win_t, wout_t, are, aim = _s5_tables(lam_re, lam_im, log_step, b_re, b_im, c_re, c_im, d_skip, n_chunks)
    n_split = 2 if bsz % 2 == 0 else 1
    rows = t // n_split
    tab = lambda a: pl.BlockSpec((1,) + a.shape[1:], lambda j, i: (j, 0, 0))
    gact = pl.pallas_call(
        functools.partial(_s5_body, n_chunks=n_chunks),
        grid=(n_oct, n_split),
        in_specs=[pl.BlockSpec((rows, LANES), lambda j, i: (i, j)), tab(dtab), tab(win_t), tab(wout_t), tab(are), tab(aim)],
        out_specs=pl.BlockSpec((rows, LANES), lambda j, i: (i, j)),
        out_shape=jax.ShapeDtypeStruct((t, width), F32),
        compiler_params=_params(("parallel", "parallel")),
        name="s5_scan",
    )(u, dtab, win_t, wout_t, are, aim)
    return _matmul_glu_residual(gact, glu_proj.astype(BF16), h, 512, 1024, "s5_glu_proj")


def _router_body(h_ref, nw_ref, rw_ref, rb_ref, idx_ref, gate_ref, *, n_experts):
    xn = _rms(h_ref[...], nw_ref[...])
    logits = jnp.dot(xn, rw_ref[...], precision=HIGHEST, preferred_element_type=F32) + rb_ref[...]
    tm = logits.shape[0]
    lane = lax.broadcasted_iota(jnp.int32, (tm, LANES), 1)
    lane_f = lane.astype(F32)
    neg_inf = jnp.float32(-jnp.inf)
    cur = jnp.where(lane < n_experts, logits, neg_inf)
    tops, idxs = [], []
    for _ in range(TOP_K):
        mx = jnp.max(cur, axis=-1, keepdims=True)
        ix = jnp.min(jnp.where(cur == mx, lane_f, float(LANES)), axis=-1, keepdims=True)
        cur = jnp.where(lane_f == ix, neg_inf, cur)
        tops.append(mx)
        idxs.append(ix)
    es = [jnp.exp(v - tops[0]) for v in tops]
    denom = es[0] + es[1] + es[2] + es[3]
    idx_out = jnp.zeros((tm, LANES), F32)
    gate_out = jnp.zeros((tm, LANES), F32)
    for k in range(TOP_K):
        idx_out = jnp.where(lane == k, idxs[k], idx_out)
        gate_out = jnp.where(lane == k, es[k] / denom, gate_out)
    idx_ref[...] = idx_out.astype(jnp.int32)
    gate_ref[...] = gate_out


def _rank_body(idx_ref, rank_ref, cnt_ref, carry_ref):
    i = pl.program_id(0)
    tb = idx_ref.shape[0]

    @pl.when(i == 0)
    def _():
        carry_ref[...] = jnp.zeros(carry_ref.shape, F32)

    idx = idx_ref[...]
    lane = lax.broadcasted_iota(jnp.int32, (tb, LANES), 1)
    onehots = [(lane == idx[:, k:k + 1]) for k in range(TOP_K)]
    hits = jnp.zeros((tb, LANES), F32)
    for oh in onehots:
        hits = hits + oh.astype(F32)
    rows = lax.broadcasted_iota(jnp.int32, (tb, tb), 0)
    cols = lax.broadcasted_iota(jnp.int32, (tb, tb), 1)
    strict = (rows > cols).astype(BF16)
    before = jnp.dot(strict, hits.astype(BF16), preferred_element_type=F32) + carry_ref[0:1, :]
    out = jnp.zeros((tb, LANES), jnp.int32)
    for k, oh in enumerate(onehots):
        rk = jnp.sum(jnp.where(oh, before, 0.0), axis=-1, keepdims=True)
        out = jnp.where(lane == k, rk.astype(jnp.int32), out)
    rank_ref[...] = out
    total = carry_ref[0:1, :] + jnp.sum(hits, axis=0, keepdims=True)
    carry_ref[...] = jnp.broadcast_to(total, carry_ref.shape)
    cnt_ref[...] = jnp.broadcast_to(total, cnt_ref.shape)


def _dispatch_body(dest_ref, rib_ref, h_ref, nw_ref, xs_ref, buf_ref, zero_ref, sem, zero_sem):
    tt, d = h_ref.shape

    @pl.when(pl.program_id(0) == 0)
    def _():
        zero_ref[...] = jnp.zeros(zero_ref.shape, zero_ref.dtype)

        def zero_copy(b, j):
            return pltpu.make_async_copy(zero_ref, xs_ref.at[pl.ds(b * MOE_TM + j * MOE_SUB, MOE_SUB), :], zero_sem)

        def zero_block(op):
            def body(b, carry):
                rows_in_block = rib_ref[b]
                for j in range(MOE_TM // MOE_SUB):
                    @pl.when(rows_in_block < (j + 1) * MOE_SUB)
                    def _():
                        op(zero_copy(b, j))
                return carry
            return body

        lax.fori_loop(0, rib_ref.shape[0], zero_block(lambda cp: cp.start()), 0)
        lax.fori_loop(0, rib_ref.shape[0], zero_block(lambda cp: cp.wait()), 0)

    xn = _rms(h_ref[...], nw_ref[...])
    buf_ref[...] = _pack_bf16_pair(xn[:, :d // 2], xn[:, d // 2:])

    def row_copy(i, k):
        d = dest_ref[0, 0, i * TOP_K + k]
        return pltpu.make_async_copy(buf_ref.at[pl.ds(i, 1), :], xs_ref.at[pl.ds(d, 1), :], sem)

    def start(i, carry):
        for k in range(TOP_K):
            row_copy(i, k).start()
        return carry

    def wait(i, carry):
        for k in range(TOP_K):
            row_copy(i, k).wait()
        return carry

    lax.fori_loop(0, tt, start, 0)
    lax.fori_loop(0, tt, wait, 0)


def _combine_body(dest_ref, h_ref, gate_ref, nw_ref, ys_ref, *rest, final):
    if final:
        o_ref, buf_ref, sem = rest
    else:
        o_ref, on_ref, buf_ref, sem = rest
    tt = h_ref.shape[0]

    def row_copy(i, k):
        d = dest_ref[0, 0, i * TOP_K + k]
        return pltpu.make_async_copy(ys_ref.at[pl.ds(d, 1), :], buf_ref.at[k, pl.ds(i, 1), :], sem)

    def start(i, carry):
        for k in range(TOP_K):
            row_copy(i, k).start()
        return carry

    def wait(i, carry):
        for k in range(TOP_K):
            row_copy(i, k).wait()
        return carry

    lax.fori_loop(0, tt, start, 0)
    lax.fori_loop(0, tt, wait, 0)
    gates = gate_ref[...]
    h = h_ref[...]
    half = h.shape[1] // 2
    acc_lo, acc_hi = h[:, :half], h[:, half:]
    for k in range(TOP_K):
        lo, hi = _unpack_bf16_pair(buf_ref[k])
        acc_lo = acc_lo + gates[:, k:k + 1] * lo
        acc_hi = acc_hi + gates[:, k:k + 1] * hi
    acc = jnp.concatenate([acc_lo, acc_hi], axis=1)
    if final:
        o_ref[...] = _rms(acc, nw_ref[...])
    else:
        o_ref[...] = acc
        on_ref[...] = _rms(acc, nw_ref[...]).astype(on_ref.dtype)


def _expert_body(be_ref, ns_ref, nu_ref, x_ref, wg_ref, wu_ref, wd_ref, bg_ref, bu_ref, bd_ref, o_ref, acc_ref):
    del be_ref, nu_ref
    b = pl.program_id(0)
    f = pl.program_id(1)
    n_sub = ns_ref[b]
    d = acc_ref.shape[1]

    @pl.when(f == 0)
    def _():
        acc_ref[...] = jnp.broadcast_to(bd_ref[0, 0], acc_ref.shape)

    for ns in range(1, MOE_TM // MOE_SUB + 1):
        @pl.when(n_sub == ns)
        def _():
            rows = ns * MOE_SUB
            lo, hi = _unpack_bf16_pair(x_ref[0:rows, :])
            x = jnp.concatenate([lo.astype(BF16), hi.astype(BF16)], axis=1)
            gate = jnp.dot(x, wg_ref[0, 0].astype(BF16), preferred_element_type=F32) + bg_ref[0, 0]
            up = jnp.dot(x, wu_ref[0, 0].astype(BF16), preferred_element_type=F32) + bu_ref[0, 0]
            gate = jnp.minimum(gate, SWIGLU_LIMIT)
            up = jnp.clip(up, -SWIGLU_LIMIT, SWIGLU_LIMIT)
            act = ((up + 1.0) * (gate * _sigmoid(SWIGLU_ALPHA * gate))).astype(BF16)
            for half in range(2):
                cs = slice(half * d // 2, (half + 1) * d // 2)
                wd = wd_ref[0, 0, :, cs].astype(BF16)
                acc_ref[0:rows, cs] = acc_ref[0:rows, cs] + jnp.dot(act, wd, preferred_element_type=F32)

    @pl.when(f == pl.num_programs(1) - 1)
    def _():
        o_ref[...] = _pack_bf16_pair(acc_ref[:, :d // 2], acc_ref[:, d // 2:])


def _moe(h, norm_w, router_w, router_b, w_gate_up, b_gate_up, w_down, b_down, layer, next_norm_w, final):
    t, d = h.shape
    n_e = router_w.shape[1]
    d_ff = w_down.shape[2]
    nw = norm_w.reshape(1, d)
    tr = min(512, t)
    rw = jnp.pad(router_w, ((0, 0), (0, LANES - n_e)))
    rb = jnp.pad(router_b, (0, LANES - n_e)).reshape(1, LANES)
    idx, gates = pl.pallas_call(
        functools.partial(_router_body, n_experts=n_e),
        grid=(t // tr,),
        in_specs=[pl.BlockSpec((tr, d), lambda i: (i, 0)), pl.BlockSpec((1, d), lambda i: (0, 0)),
                  pl.BlockSpec((d, LANES), lambda i: (0, 0)), pl.BlockSpec((1, LANES), lambda i: (0, 0))],
        out_specs=[pl.BlockSpec((tr, LANES), lambda i: (i, 0)), pl.BlockSpec((tr, LANES), lambda i: (i, 0))],
        out_shape=[jax.ShapeDtypeStruct((t, LANES), jnp.int32), jax.ShapeDtypeStruct((t, LANES), F32)],
        compiler_params=_params(("parallel",)),
        name="moe_router",
    )(h, nw, rw, rb)
    tb = min(512, t)
    rank, cnt = pl.pallas_call(
        _rank_body,
        grid=(t // tb,),
        in_specs=[pl.BlockSpec((tb, LANES), lambda i: (i, 0))],
        out_specs=[pl.BlockSpec((tb, LANES), lambda i: (i, 0)), pl.BlockSpec((SUBLANES, LANES), lambda i: (0, 0))],
        out_shape=[jax.ShapeDtypeStruct((t, LANES), jnp.int32), jax.ShapeDtypeStruct((SUBLANES, LANES), F32)],
        scratch_shapes=[pltpu.VMEM((SUBLANES, LANES), F32)],
        compiler_params=_params(("arbitrary",)),
        name="moe_rank",
    )(idx)
    counts = cnt[0, :n_e].astype(jnp.int32)
    n_blocks = (t * TOP_K) // MOE_TM + n_e
    blocks_e = (counts + MOE_TM - 1) // MOE_TM
    blk_end = jnp.cumsum(blocks_e)
    blk_start = blk_end - blocks_e
    n_used = blk_end[-1]
    bid = jnp.arange(n_blocks, dtype=jnp.int32)
    blk_expert = jnp.minimum(jnp.searchsorted(blk_end, bid, side='right'), n_e - 1).astype(jnp.int32)
    last_used_expert = blk_expert[jnp.maximum(n_used - 1, 0)]
    blk_expert = jnp.where(bid < n_used, blk_expert, last_used_expert)
    rows_in_blk = jnp.clip(counts[blk_expert] - (bid - blk_start[blk_expert]) * MOE_TM, 0, MOE_TM)
    n_sub = jnp.where(bid < n_used, (rows_in_blk + MOE_SUB - 1) // MOE_SUB, 0).astype(jnp.int32)
    dest = blk_start[idx[:, :TOP_K]] * MOE_TM + rank[:, :TOP_K]
    tt = min(256, t)
    dest3 = dest.reshape(t // tt, 1, tt * TOP_K)
    n_rows = n_blocks * MOE_TM
    dp = d // 2
    xs = pl.pallas_call(
        _dispatch_body,
        grid=(t // tt,),
        in_specs=[pl.BlockSpec((1, 1, tt * TOP_K), lambda i: (i, 0, 0), memory_space=pltpu.SMEM),
                  pl.BlockSpec(memory_space=pltpu.SMEM),
                  pl.BlockSpec((tt, d), lambda i: (i, 0)), pl.BlockSpec((1, d), lambda i: (0, 0))],
        out_specs=pl.BlockSpec(memory_space=pl.ANY),
        out_shape=jax.ShapeDtypeStruct((n_rows, dp), jnp.uint32),
        scratch_shapes=[pltpu.VMEM((tt, dp), jnp.uint32), pltpu.VMEM((MOE_SUB, dp), jnp.uint32),
                        pltpu.SemaphoreType.DMA, pltpu.SemaphoreType.DMA],
        compiler_params=_params(("arbitrary",)),
        name="moe_dispatch",
    )(dest3, jnp.where(bid < n_used, rows_in_blk, 0).astype(jnp.int32), h, nw)
    nf = d_ff // MOE_TF
    last_f = nf - 1
    used = lambda b, f, nu: jnp.where(b < nu[0], f, last_f)
    ys = pl.pallas_call(
        _expert_body,
        grid_spec=pltpu.PrefetchScalarGridSpec(
            num_scalar_prefetch=3,
            grid=(n_blocks, nf),
            in_specs=[
                pl.BlockSpec((MOE_TM, dp), lambda b, f, be, ns, nu: (jnp.minimum(b, jnp.maximum(nu[0] - 1, 0)), 0)),
                pl.BlockSpec((1, 1, d, MOE_TF), lambda b, f, be, ns, nu: (layer, be[b], 0, used(b, f, nu))),
                pl.BlockSpec((1, 1, d, MOE_TF), lambda b, f, be, ns, nu: (layer, be[b], 0, nf + used(b, f, nu))),
                pl.BlockSpec((1, 1, MOE_TF, d), lambda b, f, be, ns, nu: (layer, be[b], used(b, f, nu), 0)),
                pl.BlockSpec((1, 1, 1, MOE_TF), lambda b, f, be, ns, nu: (layer, be[b], 0, used(b, f, nu))),
                pl.BlockSpec((1, 1, 1, MOE_TF), lambda b, f, be, ns, nu: (layer, be[b], 0, nf + used(b, f, nu))),
                pl.BlockSpec((1, 1, 1, d), lambda b, f, be, ns, nu: (layer, be[b], 0, 0)),
            ],
            out_specs=pl.BlockSpec((MOE_TM, dp), lambda b, f, be, ns, nu: (b, 0)),
            scratch_shapes=[pltpu.VMEM((MOE_TM, d), F32)],
        ),
        out_shape=jax.ShapeDtypeStruct((n_rows, dp), jnp.uint32),
        compiler_params=_params(("arbitrary", "arbitrary")),
        name="moe_experts",
    )(blk_expert, n_sub, n_used.reshape(1).astype(jnp.int32), xs, w_gate_up, w_gate_up, w_down,
      b_gate_up.reshape(-1, n_e, 1, 2 * d_ff), b_gate_up.reshape(-1, n_e, 1, 2 * d_ff), b_down.reshape(-1, n_e, 1, d))
    tc = min(256, t)
    destc = dest.reshape(t // tc, 1, tc * TOP_K)
    row_spec = pl.BlockSpec((tc, d), lambda i: (i, 0))
    out_specs, out_shape = row_spec, jax.ShapeDtypeStruct((t, d), F32)
    if not final:
        out_specs, out_shape = [row_spec, row_spec], [out_shape, jax.ShapeDtypeStruct((t, d), BF16)]
    return pl.pallas_call(
        functools.partial(_combine_body, final=final),
        grid=(t // tc,),
        in_specs=[pl.BlockSpec((1, 1, tc * TOP_K), lambda i: (i, 0, 0), memory_space=pltpu.SMEM),
                  row_spec, pl.BlockSpec((tc, LANES), lambda i: (i, 0)), pl.BlockSpec((1, d), lambda i: (0, 0)),
                  pl.BlockSpec(memory_space=pl.ANY)],
        out_specs=out_specs,
        out_shape=out_shape,
        scratch_shapes=[pltpu.VMEM((TOP_K, tc, dp), jnp.uint32), pltpu.SemaphoreType.DMA],
        compiler_params=_params(("arbitrary",)),
        name="moe_combine",
    )(destc, h, gates, next_norm_w.reshape(1, d), ys)


def kernel(x, norm_mix_w, norm_ffn_w, final_norm_w, a_in_proj, a_conv_w, a_conv_b, a_dt_bias, a_log, a_d_skip, a_norm_w, a_out_proj, b_in_proj, b_lam_re, b_lam_im, b_log_step, b_b_re, b_b_im, b_c_re, b_c_im, b_d_skip, b_glu_proj, router_w, router_b, w_gate_up, b_gate_up, w_down, b_down):
    bsz, seq, d = x.shape
    depth = norm_mix_w.shape[0]
    h = x.reshape(bsz * seq, d)
    xn = _rmsnorm(h, norm_mix_w[0], BF16)
    for i in range(depth):
        j = i // 2
        if i % 2 == 0:
            h = _ssd_mixer(h, xn, a_in_proj[j], a_conv_w[j], a_conv_b[j], a_dt_bias[j], a_log[j], a_d_skip[j],
                           a_norm_w[j], a_out_proj[j], bsz, seq)
        else:
            h = _s5_mixer(h, xn, b_in_proj[j], b_lam_re[j], b_lam_im[j], b_log_step[j], b_b_re[j], b_b_im[j],
                          b_c_re[j], b_c_im[j], b_d_skip[j], b_glu_proj[j], bsz, seq)
        if i == depth - 1:
            out = _moe(h, norm_ffn_w[i], router_w[i], router_b[i], w_gate_up, b_gate_up, w_down, b_down, i,
                       final_norm_w, True)
            return out.reshape(bsz, seq, d)
        h, xn = _moe(h, norm_ffn_w[i], router_w[i], router_b[i], w_gate_up, b_gate_up, w_down, b_down, i,
                     norm_mix_w[i + 1], False)
```

```python
import functools
import math

import jax
import jax.numpy as jnp
from jax import lax
from jax.experimental import pallas as pl
from jax.experimental.pallas import tpu as pltpu

F32 = jnp.float32
BF16 = jnp.bfloat16
HIGHEST = lax.Precision.HIGHEST

RMS_EPS = 1e-6
LOG2_E = 1.0 / math.log(2.0)
LANES = 128
SUBLANES = 8
V7X_VMEM_LIMIT_BYTES = 58 * 1024 * 1024

SSD_HEAD_DIM = 64
SSD_GROUPS = 8
SSD_HEADS_PER_GROUP = 8
SSD_STATE = 128
SSD_CONV = 4
SSD_CHUNK = 256
SSD_GROUP_W = SSD_HEADS_PER_GROUP * SSD_HEAD_DIM
SSD_PAIRS = SSD_GROUP_W // LANES

S5_GROUP_CH = 16
S5_STATE = 64
S5_CHUNK = 16
S5_OCT = LANES // S5_GROUP_CH

TOP_K = 4
SWIGLU_LIMIT = 7.0
SWIGLU_ALPHA = 1.702
MOE_TM = 1024
MOE_SUB = 256
MOE_TF = 256


def _params(semantics):
    return pltpu.CompilerParams(dimension_semantics=semantics, vmem_limit_bytes=V7X_VMEM_LIMIT_BYTES)


def _sigmoid(x):
    return 0.5 + 0.5 * jnp.tanh(0.5 * x)


def _silu(x):
    hx = 0.5 * x
    return hx + hx * jnp.tanh(hx)


def _pack_bf16_pair(lo, hi):
    ulo = lax.bitcast_convert_type(lo.astype(BF16).astype(F32), jnp.uint32)
    uhi = lax.bitcast_convert_type(hi.astype(BF16).astype(F32), jnp.uint32)
    return (ulo >> 16) | uhi


def _unpack_bf16_pair(w):
    lo = lax.bitcast_convert_type(w << 16, F32)
    hi = lax.bitcast_convert_type(w & jnp.uint32(0xFFFF0000), F32)
    return lo, hi


def _rms(x, w):
    ms = jnp.mean(x * x, axis=-1, keepdims=True)
    return x * lax.rsqrt(ms + RMS_EPS) * w


def _rmsnorm_body(x_ref, w_ref, o_ref):
    o_ref[...] = _rms(x_ref[...], w_ref[...]).astype(o_ref.dtype)


def _rmsnorm(x, w, out_dtype, tm=512):
    t, d = x.shape
    tm = min(tm, t)
    return pl.pallas_call(
        _rmsnorm_body,
        grid=(t // tm,),
        in_specs=[pl.BlockSpec((tm, d), lambda i: (i, 0)), pl.BlockSpec((1, d), lambda i: (0, 0))],
        out_specs=pl.BlockSpec((tm, d), lambda i: (i, 0)),
        out_shape=jax.ShapeDtypeStruct((t, d), out_dtype),
        compiler_params=_params(("parallel",)),
        name="rmsnorm",
    )(x, w.reshape(1, d))


def _mm_body(x_ref, w_ref, o_ref):
    o_ref[...] = jnp.dot(x_ref[...], w_ref[...], preferred_element_type=F32).astype(o_ref.dtype)


def _mm_res_body(x_ref, w_ref, r_ref, o_ref):
    o_ref[...] = r_ref[...] + jnp.dot(x_ref[...], w_ref[...], preferred_element_type=F32)


def _mm_glu_body(x_ref, wv_ref, wg_ref, r_ref, o_ref):
    x = x_ref[...].astype(BF16)
    v = jnp.dot(x, wv_ref[...], preferred_element_type=F32)
    g = jnp.dot(x, wg_ref[...], preferred_element_type=F32)
    o_ref[...] = r_ref[...] + v * _sigmoid(g)


def _matmul(x, w, out_dtype, tm, tn, name):
    m, k = x.shape
    n = w.shape[1]
    tm, tn = min(tm, m), min(tn, n)
    return pl.pallas_call(
        _mm_body,
        grid=(m // tm, n // tn),
        in_specs=[pl.BlockSpec((tm, k), lambda i, j: (i, 0)), pl.BlockSpec((k, tn), lambda i, j: (0, j))],
        out_specs=pl.BlockSpec((tm, tn), lambda i, j: (i, j)),
        out_shape=jax.ShapeDtypeStruct((m, n), out_dtype),
        compiler_params=_params(("parallel", "parallel")),
        name=name,
    )(x, w)


def _matmul_residual(x, w, res, tm, tn, name):
    m, k = x.shape
    n = w.shape[1]
    tm, tn = min(tm, m), min(tn, n)
    return pl.pallas_call(
        _mm_res_body,
        grid=(m // tm, n // tn),
        in_specs=[pl.BlockSpec((tm, k), lambda i, j: (i, 0)), pl.BlockSpec((k, tn), lambda i, j: (0, j)),
                  pl.BlockSpec((tm, tn), lambda i, j: (i, j))],
        out_specs=pl.BlockSpec((tm, tn), lambda i, j: (i, j)),
        out_shape=jax.ShapeDtypeStruct((m, n), F32),
        compiler_params=_params(("parallel", "parallel")),
        name=name,
    )(x, w, res)


def _matmul_glu_residual(x, w, res, tm, tn, name):
    m, k = x.shape
    n = w.shape[1] // 2
    tm, tn = min(tm, m), min(tn, n)
    nj = n // tn
    return pl.pallas_call(
        _mm_glu_body,
        grid=(m // tm, nj),
        in_specs=[pl.BlockSpec((tm, k), lambda i, j: (i, 0)),
                  pl.BlockSpec((k, tn), lambda i, j: (0, j)),
                  pl.BlockSpec((k, tn), lambda i, j: (0, nj + j)),
                  pl.BlockSpec((tm, tn), lambda i, j: (i, j))],
        out_specs=pl.BlockSpec((tm, tn), lambda i, j: (i, j)),
        out_shape=jax.ShapeDtypeStruct((m, n), F32),
        compiler_params=_params(("parallel", "parallel")),
        name=name,
    )(x, w, w, res)


def _ssd_body(z_ref, x_ref, b_ref, c_ref, dt_ref,
              cwx_ref, cwb_ref, cwc_ref, cbx_ref, cbb_ref, cbc_ref,
              dtb_ref, alog_ref, dexp_ref, nw_ref,
              o_ref,
              tx_ref, tb_ref, tc_ref, st_ref, pad_ref):
    c = pl.program_id(1)
    g = pl.program_id(2)
    lc = x_ref.shape[1]

    @pl.when(c == 0)
    def _():
        tx_ref[g] = jnp.zeros(tx_ref.shape[1:], F32)
        tb_ref[g] = jnp.zeros(tb_ref.shape[1:], F32)
        tc_ref[g] = jnp.zeros(tc_ref.shape[1:], F32)
        st_ref[g] = jnp.zeros(st_ref.shape[1:], F32)

    def conv_silu(u_ref, w_ref, bias_ref, tail_ref):
        width = u_ref.shape[2]
        u = u_ref[0].astype(F32)
        pad_ref[0:SUBLANES, 0:width] = tail_ref[g]
        pad_ref[SUBLANES:2 * SUBLANES, 0:width] = u[0:SUBLANES, :]
        tail_ref[g] = u[lc - SUBLANES:lc, :]
        acc = bias_ref[...] + w_ref[SSD_CONV - 1:SSD_CONV, :] * u
        for k in range(SSD_CONV - 1):
            sh = SSD_CONV - 1 - k
            body = pltpu.roll(u, sh, 0)
            head = pad_ref[SUBLANES - sh:2 * SUBLANES - sh, 0:width]
            acc = acc + w_ref[k:k + 1, :] * jnp.concatenate([head, body[SUBLANES:, :]], axis=0)
        return _silu(acc)

    xg = conv_silu(x_ref, cwx_ref, cbx_ref, tx_ref)
    bg = conv_silu(b_ref, cwb_ref, cbb_ref, tb_ref)
    cg = conv_silu(c_ref, cwc_ref, cbc_ref, tc_ref)

    dt_in = dt_ref[0] + dtb_ref[0]
    dt = jnp.maximum(dt_in, 0.0) + jnp.log1p(jnp.exp(-jnp.abs(dt_in)))
    a = dt * (-jnp.exp(alog_ref[0]))
    rows = lax.broadcasted_iota(jnp.int32, (lc, lc), 0)
    cols = lax.broadcasted_iota(jnp.int32, (lc, lc), 1)
    a_cum = jnp.dot((rows >= cols).astype(F32), a, precision=HIGHEST, preferred_element_type=F32)
    a2 = a_cum * LOG2_E
    a2_t = a2.T
    src_t = a2_t - jnp.log(dt.T) * LOG2_E

    bgt = bg.T
    cb = jnp.dot(cg.astype(BF16), bgt.astype(BF16), preferred_element_type=F32)
    left = lax.broadcasted_iota(jnp.int32, (1, LANES), 1) < SSD_HEAD_DIM
    n_blk = lc // LANES
    diag_mask = (lax.broadcasted_iota(jnp.int32, (LANES, LANES), 0) >= lax.broadcasted_iota(jnp.int32, (LANES, LANES), 1))
    neg_inf = jnp.float32(-jnp.inf)

    ys = []
    for p in range(SSD_PAIRS):
        xp = xg[:, p * LANES:(p + 1) * LANES]
        xpb = xp.astype(BF16)
        st = st_ref[g, p]
        stb = st.astype(BF16)
        y_h, st_h = [], []
        for hh in range(2):
            r = 2 * p + hh
            colb = jnp.broadcast_to(a2[:, r:r + 1], (lc, LANES))
            srow = src_t[r:r + 1, :]
            a2_tot = a2_t[r:r + 1, lc - 1:lc]
            cs = (cg * jnp.exp2(colb)).astype(BF16)
            y_rows = []
            for i in range(n_blk):
                ci = colb[i * LANES:(i + 1) * LANES]
                parts = []
                for j in range(i + 1):
                    seg = ci - srow[:, j * LANES:(j + 1) * LANES]
                    if j == i:
                        seg = jnp.where(diag_mask, seg, neg_inf)
                    parts.append((cb[i * LANES:(i + 1) * LANES, j * LANES:(j + 1) * LANES] * jnp.exp2(seg)).astype(BF16))
                lhs = jnp.concatenate(parts + [cs[i * LANES:(i + 1) * LANES]], axis=1)
                rhs = jnp.concatenate([xpb[:(i + 1) * LANES], stb], axis=0)
                y_rows.append(jnp.dot(lhs, rhs, preferred_element_type=F32))
            y_h.append(jnp.concatenate(y_rows, axis=0))
            w_row = jnp.exp2(a2_tot - srow)
            s_new = jnp.dot((bgt * w_row).astype(BF16), xpb, preferred_element_type=F32)
            st_h.append(st * jnp.exp2(a2_tot) + s_new)
        ys.append(jnp.where(left, y_h[0], y_h[1]) + dexp_ref[:, p * LANES:(p + 1) * LANES] * xp)
        st_ref[g, p] = jnp.where(left, st_h[0], st_h[1])

    y = jnp.concatenate(ys, axis=1)
    yz = y * _silu(z_ref[0].astype(F32))
    ms = jnp.mean(yz * yz, axis=-1, keepdims=True)
    o_ref[0] = (yz * lax.rsqrt(ms + RMS_EPS) * nw_ref[...]).astype(o_ref.dtype)


def _ssd(zx, dt, conv_w, conv_b, dt_bias, a_log, d_skip, norm_w):
    bsz, seq, _ = zx.shape
    g_n, gw, n = SSD_GROUPS, SSD_GROUP_W, SSD_STATE
    d_inner = g_n * gw
    lc = min(SSD_CHUNK, seq)
    nc = seq // lc
    xb0 = d_inner // gw
    bb0 = 2 * d_inner // n
    cb0 = bb0 + g_n
    cwx, cwb, cwc = conv_w[:, :d_inner], conv_w[:, d_inner:d_inner + g_n * n], conv_w[:, d_inner + g_n * n:]
    cb_ = conv_b.reshape(1, -1)
    cbx, cbb, cbc = cb_[:, :d_inner], cb_[:, d_inner:d_inner + g_n * n], cb_[:, d_inner + g_n * n:]
    pad_heads = lambda v: jnp.pad(v.reshape(g_n, 1, SSD_HEADS_PER_GROUP), ((0, 0), (0, 0), (0, LANES - SSD_HEADS_PER_GROUP)))
    dexp = jnp.repeat(d_skip, SSD_HEAD_DIM).reshape(1, d_inner)
    grp = lambda w: pl.BlockSpec((1, lc, w), lambda b, c, g: (b, c, g))
    return pl.pallas_call(
        _ssd_body,
        grid=(bsz, nc, g_n),
        in_specs=[
            pl.BlockSpec((1, lc, gw), lambda b, c, g: (b, c, g)),
            pl.BlockSpec((1, lc, gw), lambda b, c, g: (b, c, xb0 + g)),
            pl.BlockSpec((1, lc, n), lambda b, c, g: (b, c, bb0 + g)),
            pl.BlockSpec((1, lc, n), lambda b, c, g: (b, c, cb0 + g)),
            pl.BlockSpec((1, lc, LANES), lambda b, c, g: (b, c, g)),
            pl.BlockSpec((SSD_CONV, gw), lambda b, c, g: (0, g)),
            pl.BlockSpec((SSD_CONV, n), lambda b, c, g: (0, g)),
            pl.BlockSpec((SSD_CONV, n), lambda b, c, g: (0, g)),
            pl.BlockSpec((1, gw), lambda b, c, g: (0, g)),
            pl.BlockSpec((1, n), lambda b, c, g: (0, g)),
            pl.BlockSpec((1, n), lambda b, c, g: (0, g)),
            pl.BlockSpec((1, 1, LANES), lambda b, c, g: (g, 0, 0)),
            pl.BlockSpec((1, 1, LANES), lambda b, c, g: (g, 0, 0)),
            pl.BlockSpec((1, gw), lambda b, c, g: (0, g)),
            pl.BlockSpec((1, gw), lambda b, c, g: (0, g)),
        ],
        out_specs=pl.BlockSpec((1, lc, gw), lambda b, c, g: (b, c, g)),
        out_shape=jax.ShapeDtypeStruct((bsz, seq, d_inner), BF16),
        scratch_shapes=[
            pltpu.VMEM((g_n, SUBLANES, gw), F32),
            pltpu.VMEM((g_n, SUBLANES, n), F32),
            pltpu.VMEM((g_n, SUBLANES, n), F32),
            pltpu.VMEM((g_n, SSD_PAIRS, n, LANES), F32),
            pltpu.VMEM((2 * SUBLANES, gw), F32),
        ],
        compiler_params=_params(("arbitrary", "arbitrary", "arbitrary")),
        name="ssd_scan",
    )(zx, zx, zx, zx, dt, cwx, cwb, cwc, cbx, cbb, cbc, pad_heads(dt_bias), pad_heads(a_log), dexp,
      norm_w.reshape(1, d_inner))


def _ssd_mixer(h, xn, in_proj, conv_w, conv_b, dt_bias, a_log, d_skip, norm_w, out_proj, bsz, seq):
    t, d = h.shape
    n_heads = dt_bias.shape[0]
    d_main = in_proj.shape[1] - n_heads
    w_main = in_proj[:, :d_main].astype(BF16)
    w_dt = jnp.pad(in_proj[:, d_main:], ((0, 0), (0, LANES - n_heads))).astype(BF16)
    zx = _matmul(xn, w_main, BF16, 1024, 1024, "ssd_in_proj")
    dt_raw = _matmul(xn, w_dt, F32, 1024, LANES, "ssd_dt_proj")
    dt = dt_raw[:, :n_heads].reshape(t, SSD_GROUPS, SSD_HEADS_PER_GROUP)
    dt = jnp.pad(dt, ((0, 0), (0, 0), (0, LANES - SSD_HEADS_PER_GROUP))).reshape(bsz, seq, SSD_GROUPS * LANES)
    y = _ssd(zx.reshape(bsz, seq, d_main), dt, conv_w, conv_b, dt_bias, a_log, d_skip, norm_w)
    return _matmul_residual(y.reshape(t, -1), out_proj.astype(BF16), h, 512, 1024, "ssd_out_proj")


def _s5_body(u_ref, d_ref, win_ref, wout_ref, are_ref, aim_ref, o_ref, *, n_chunks):
    r = u_ref.shape[0] // S5_CHUNK
    pw = 2 * LANES
    n_pairs = S5_CHUNK // 2
    n_state = are_ref.shape[2]
    z = jnp.concatenate([u_ref[pl.ds(s, r, stride=S5_CHUNK), :] for s in range(S5_CHUNK)], axis=1).astype(BF16)
    xin = jnp.dot(z, win_ref[0], preferred_element_type=F32)
    xre, xim = xin[:, :n_state], xin[:, n_state:]
    kidx = lax.broadcasted_iota(jnp.int32, (r, n_state), 0) % n_chunks

    def shifted(v, sh):
        return jnp.where(kidx >= sh, pltpu.roll(v, sh, 0), 0.0)

    sh, lvl = 1, 0
    while sh < n_chunks:
        are, aim = are_ref[0, lvl:lvl + 1, :], aim_ref[0, lvl:lvl + 1, :]
        sre, sim = shifted(xre, sh), shifted(xim, sh)
        xre, xim = xre + are * sre - aim * sim, xim + are * sim + aim * sre
        sh, lvl = sh * 2, lvl + 1
    prev = jnp.concatenate([shifted(xre, 1), shifted(xim, 1)], axis=1).astype(BF16)
    carry = jnp.dot(prev, wout_ref[0], preferred_element_type=F32)
    for tp in range(n_pairs):
        y2 = jnp.dot(z[:, :(tp + 1) * pw], d_ref[0, (n_pairs - 1 - tp) * pw:, :], preferred_element_type=F32)
        y2 = y2 + carry[:, tp * pw:(tp + 1) * pw]
        gelu = 0.5 * y2 * (1.0 + jnp.tanh(math.sqrt(2.0 / math.pi) * (y2 + 0.044715 * (y2 * y2 * y2))))
        for i in range(2):
            o_ref[pl.ds(2 * tp + i, r, stride=S5_CHUNK), :] = gelu[:, i * LANES:(i + 1) * LANES]


def _s5_tables(lam_re, lam_im, log_step, b_re, b_im, c_re, c_im, d_skip, n_chunks):
    g_n, p_n = lam_re.shape
    ch, lcs, oct_ = S5_GROUP_CH, S5_CHUNK, S5_OCT
    n_oct = g_n // oct_
    lam = lax.complex(lam_re, lam_im)
    step = jnp.exp(log_step)[:, None]
    lam_bar = jnp.exp(lam * step)
    b_bar = ((lam_bar - 1.0) / lam)[..., None] * lax.complex(b_re, b_im)
    cc = lax.complex(c_re, c_im)
    taus = jnp.arange(lcs + 1, dtype=F32)
    lam_pow = jnp.exp((lam * step)[None] * taus[:, None, None])

    def spread(w, src_lane, grp_lane, g_axis):
        sel = (jnp.arange(w.shape[-1])[:, None] == src_lane[None, :]).astype(BF16)
        v = jnp.dot(w.astype(BF16), sel, preferred_element_type=F32)
        gshape = [1] * v.ndim
        gshape[g_axis] = oct_
        return jnp.where(jnp.arange(oct_).reshape(gshape) == grp_lane, v, 0.0).astype(BF16)

    kern = jnp.einsum('gcp,tgp,gpd->gtcd', cc, lam_pow[:lcs], b_bar, precision=HIGHEST).real
    kern = kern.at[:, 0].add(d_skip.reshape(g_n, ch)[:, :, None] * jnp.eye(ch, dtype=F32))
    kz = jnp.concatenate([jnp.zeros((g_n, 1, ch, ch), F32), kern], axis=1)
    dl = jnp.arange(lcs // 2)[:, None, None]
    s2 = jnp.arange(2)[None, :, None]
    t2 = jnp.arange(2)[None, None, :]
    blk = kz[:, 2 * dl + t2 - s2 + 1]
    blk = jnp.transpose(blk.reshape(n_oct, oct_, lcs // 2, 2, 2, ch, ch), (0, 2, 3, 1, 6, 4, 5))
    lane2 = jnp.arange(2 * LANES)
    dtab = spread(blk.reshape(n_oct, lcs // 2, 2, oct_, ch, 2 * ch),
                  (lane2 // LANES) * ch + lane2 % ch, (lane2 % LANES) // ch, 3)
    dtab = dtab.reshape(n_oct, lcs // 2, 2 * LANES, 2 * LANES)[:, ::-1].reshape(n_oct, lcs * LANES, 2 * LANES)
    win = jnp.einsum('sgp,gpd->gsdp', lam_pow[:lcs][::-1], b_bar)
    win = jnp.transpose(win.reshape(n_oct, oct_, lcs, ch, p_n), (0, 2, 1, 3, 4))
    n_state = oct_ * p_n
    lane_s = jnp.arange(2 * n_state)
    win_t = spread(jnp.concatenate([win.real, win.imag], axis=-1),
                   (lane_s // n_state) * p_n + lane_s % p_n, (lane_s % n_state) // p_n, 2)
    win_t = win_t.reshape(n_oct, lcs * LANES, 2 * n_state)
    wout = jnp.einsum('gcp,tgp->gptc', cc, lam_pow[1:lcs + 1]).reshape(n_oct, oct_, p_n, lcs * ch)
    lane_o = jnp.arange(lcs * LANES)
    src_o, grp_o = (lane_o // LANES) * ch + lane_o % ch, (lane_o % LANES) // ch
    wout_t = jnp.concatenate([spread(wout.real, src_o, grp_o, 1).reshape(n_oct, n_state, lcs * LANES),
                              spread(-wout.imag, src_o, grp_o, 1).reshape(n_oct, n_state, lcs * LANES)], axis=1)
    n_lvl = max(1, int(math.ceil(math.log2(max(n_chunks, 2)))))
    lvl_pow = lcs * (2.0 ** jnp.arange(n_lvl, dtype=F32))
    a_pow = jnp.exp((lam * step)[None] * lvl_pow[:, None, None])
    a_pow = jnp.transpose(a_pow.reshape(n_lvl, n_oct, oct_ * p_n), (1, 0, 2))
    a_pad = ((0, 0), (0, -n_lvl % SUBLANES), (0, 0))
    return (dtab, win_t, wout_t,
            jnp.pad(a_pow.real, a_pad), jnp.pad(a_pow.imag, a_pad))


def _s5_mixer(h, xn, in_proj, lam_re, lam_im, log_step, b_re, b_im, c_re, c_im, d_skip, glu_proj, bsz, seq):
    t, d = h.shape
    g_n, p_n = lam_re.shape
    width = g_n * S5_GROUP_CH
    n_chunks = seq // S5_CHUNK
    n_oct = g_n // S5_OCT
    u = _matmul(xn, in_proj.astype(BF16), F32, 1024, 1024, "s5_in_proj")
    dtab, win_t, wout_t, are, aim = _s5_tables(lam_re, lam_im, log_step, b_re, b_im, c_re, c_im, d_skip, n_chunks)
    n_split = 2 if bsz % 2 == 0 else 1
    rows = t // n_split
    tab = lambda a: pl.BlockSpec((1,) + a.shape[1:], lambda j, i: (j, 0, 0))
    gact = pl.pallas_call(
        functools.partial(_s5_body, n_chunks=n_chunks),
        grid=(n_oct, n_split),
        in_specs=[pl.BlockSpec((rows, LANES), lambda j, i: (i, j)), tab(dtab), tab(win_t), tab(wout_t), tab(are), tab(aim)],
        out_specs=pl.BlockSpec((rows, LANES), lambda j, i: (i, j)),
        out_shape=jax.ShapeDtypeStruct((t, width), F32),
        compiler_params=_params(("parallel", "parallel")),
        name="s5_scan",
    )(u, dtab, win_t, wout_t, are, aim)
    return _matmul_glu_residual(gact, glu_proj.astype(BF16), h, 512, 1024, "s5_glu_proj")


def _router_body(h_ref, nw_ref, rw_ref, rb_ref, idx_ref, gate_ref, *, n_experts):
    xn = _rms(h_ref[...], nw_ref[...])
    logits = jnp.dot(xn, rw_ref[...], precision=HIGHEST, preferred_element_type=F32) + rb_ref[...]
    tm = logits.shape[0]
    lane = lax.broadcasted_iota(jnp.int32, (tm, LANES), 1)
    lane_f = lane.astype(F32)
    neg_inf = jnp.float32(-jnp.inf)
    cur = jnp.where(lane < n_experts, logits, neg_inf)
    tops, idxs = [], []
    for _ in range(TOP_K):
        mx = jnp.max(cur, axis=-1, keepdims=True)
        ix = jnp.min(jnp.where(cur == mx, lane_f, float(LANES)), axis=-1, keepdims=True)
        cur = jnp.where(lane_f == ix, neg_inf, cur)
        tops.append(mx)
        idxs.append(ix)
    es = [jnp.exp(v - tops[0]) for v in tops]
    denom = es[0] + es[1] + es[2] + es[3]
    idx_out = jnp.zeros((tm, LANES), F32)
    gate_out = jnp.zeros((tm, LANES), F32)
    for k in range(TOP_K):
        idx_out = jnp.where(lane == k, idxs[k], idx_out)
        gate_out = jnp.where(lane == k, es[k] / denom, gate_out)
    idx_ref[...] = idx_out.astype(jnp.int32)
    gate_ref[...] = gate_out


def _rank_body(idx_ref, rank_ref, cnt_ref, carry_ref):
    i = pl.program_id(0)
    tb = idx_ref.shape[0]

    @pl.when(i == 0)
    def _():
        carry_ref[...] = jnp.zeros(carry_ref.shape, F32)

    idx = idx_ref[...]
    lane = lax.broadcasted_iota(jnp.int32, (tb, LANES), 1)
    onehots = [(lane == idx[:, k:k + 1]) for k in range(TOP_K)]
    hits = jnp.zeros((tb, LANES), F32)
    for oh in onehots:
        hits = hits + oh.astype(F32)
    rows = lax.broadcasted_iota(jnp.int32, (tb, tb), 0)
    cols = lax.broadcasted_iota(jnp.int32, (tb, tb), 1)
    strict = (rows > cols).astype(BF16)
    before = jnp.dot(strict, hits.astype(BF16), preferred_element_type=F32) + carry_ref[0:1, :]
    out = jnp.zeros((tb, LANES), jnp.int32)
    for k, oh in enumerate(onehots):
        rk = jnp.sum(jnp.where(oh, before, 0.0), axis=-1, keepdims=True)
        out = jnp.where(lane == k, rk.astype(jnp.int32), out)
    rank_ref[...] = out
    total = carry_ref[0:1, :] + jnp.sum(hits, axis=0, keepdims=True)
    carry_ref[...] = jnp.broadcast_to(total, carry_ref.shape)
    cnt_ref[...] = jnp.broadcast_to(total, cnt_ref.shape)


def _dispatch_body(dest_ref, rib_ref, h_ref, nw_ref, xs_ref, buf_ref, zero_ref, sems, zero_sem, *, n_steps):
    tt, d = h_ref.shape
    step = pl.program_id(0)
    slot = lax.rem(step, 2)

    @pl.when(step == 0)
    def _():
        zero_ref[...] = jnp.zeros(zero_ref.shape, zero_ref.dtype)

        def zero_copy(b, j):
            return pltpu.make_async_copy(zero_ref, xs_ref.at[pl.ds(b * MOE_TM + j * MOE_SUB, MOE_SUB), :], zero_sem)

        def zero_block(op):
            def body(b, carry):
                rows_in_block = rib_ref[b]
                for j in range(MOE_TM // MOE_SUB):
                    @pl.when(rows_in_block < (j + 1) * MOE_SUB)
                    def _():
                        op(zero_copy(b, j))
                return carry
            return body

        lax.fori_loop(0, rib_ref.shape[0], zero_block(lambda cp: cp.start()), 0)
        lax.fori_loop(0, rib_ref.shape[0], zero_block(lambda cp: cp.wait()), 0)

    def drain(s):
        for _ in range(TOP_K):
            pltpu.make_async_copy(buf_ref.at[s], xs_ref.at[pl.ds(0, tt), :], sems.at[s]).wait()

    @pl.when(step >= 2)
    def _():
        drain(slot)

    xn = _rms(h_ref[...], nw_ref[...])
    buf_ref[slot] = _pack_bf16_pair(xn[:, :d // 2], xn[:, d // 2:])

    for s in range(2):
        @pl.when(slot == s)
        def _():
            def start(i, carry):
                for k in range(TOP_K):
                    row = dest_ref[0, 0, i * TOP_K + k]
                    pltpu.make_async_copy(buf_ref.at[s, pl.ds(i, 1), :], xs_ref.at[pl.ds(row, 1), :], sems.at[s]).start()
                return carry

            lax.fori_loop(0, tt, start, 0, unroll=2)

    @pl.when(step == n_steps - 1)
    def _():
        if n_steps >= 2:
            drain(1 - slot)
        drain(slot)


def _combine_body(dest_ref, dest_next_ref, h_ref, gate_ref, nw_ref, ys_ref, *rest, final, n_steps):
    if final:
        o_ref, buf_ref, sems = rest
    else:
        o_ref, on_ref, buf_ref, sems = rest
    tt = h_ref.shape[0]
    step = pl.program_id(0)
    slot = lax.rem(step, 2)

    def issue(idx_ref, s):
        def start(i, carry):
            for k in range(TOP_K):
                row = idx_ref[0, 0, i * TOP_K + k]
                pltpu.make_async_copy(ys_ref.at[pl.ds(row, 1), :], buf_ref.at[s, pl.ds(k * tt + i, 1), :],
                                      sems.at[s]).start()
            return carry
        lax.fori_loop(0, tt, start, 0, unroll=2)

    @pl.when(step == 0)
    def _():
        issue(dest_ref, 0)

    for s in range(2):
        @pl.when((step + 1 < n_steps) & (slot == s))
        def _():
            issue(dest_next_ref, 1 - s)

    pltpu.make_async_copy(ys_ref.at[pl.ds(0, TOP_K * tt), :], buf_ref.at[slot], sems.at[slot]).wait()
    gates = gate_ref[...]
    h = h_ref[...]
    half = h.shape[1] // 2
    acc_lo, acc_hi = h[:, :half], h[:, half:]
    for k in range(TOP_K):
        lo, hi = _unpack_bf16_pair(buf_ref[slot, k * tt:(k + 1) * tt, :])
        acc_lo = acc_lo + gates[:, k:k + 1] * lo
        acc_hi = acc_hi + gates[:, k:k + 1] * hi
    acc = jnp.concatenate([acc_lo, acc_hi], axis=1)
    if final:
        o_ref[...] = _rms(acc, nw_ref[...])
    else:
        o_ref[...] = acc
        on_ref[...] = _rms(acc, nw_ref[...]).astype(on_ref.dtype)


def _expert_body(be_ref, ns_ref, nu_ref, x_ref, wg_ref, wu_ref, wd_ref, bg_ref, bu_ref, bd_ref, o_ref, acc_ref):
    del be_ref, nu_ref
    b = pl.program_id(0)
    f = pl.program_id(1)
    n_sub = ns_ref[b]
    d = acc_ref.shape[1]

    @pl.when(f == 0)
    def _():
        acc_ref[...] = jnp.broadcast_to(bd_ref[0, 0], acc_ref.shape)

    for ns in range(1, MOE_TM // MOE_SUB + 1):
        @pl.when(n_sub == ns)
        def _():
            rows = ns * MOE_SUB
            lo, hi = _unpack_bf16_pair(x_ref[0:rows, :])
            x = jnp.concatenate([lo.astype(BF16), hi.astype(BF16)], axis=1)
            gate = jnp.dot(x, wg_ref[0, 0].astype(BF16), preferred_element_type=F32) + bg_ref[0, 0]
            up = jnp.dot(x, wu_ref[0, 0].astype(BF16), preferred_element_type=F32) + bu_ref[0, 0]
            gate = jnp.minimum(gate, SWIGLU_LIMIT)
            up = jnp.clip(up, -SWIGLU_LIMIT, SWIGLU_LIMIT)
            act = ((up + 1.0) * (gate * _sigmoid(SWIGLU_ALPHA * gate))).astype(BF16)
            for half in range(2):
                cs = slice(half * d // 2, (half + 1) * d // 2)
                wd = wd_ref[0, 0, :, cs].astype(BF16)
                acc_ref[0:rows, cs] = acc_ref[0:rows, cs] + jnp.dot(act, wd, preferred_element_type=F32)

    @pl.when(f == pl.num_programs(1) - 1)
    def _():
        o_ref[...] = _pack_bf16_pair(acc_ref[:, :d // 2], acc_ref[:, d // 2:])


def _moe(h, norm_w, router_w, router_b, w_gate_up, b_gate_up, w_down, b_down, layer, next_norm_w, final):
    t, d = h.shape
    n_e = router_w.shape[1]
    d_ff = w_down.shape[2]
    nw = norm_w.reshape(1, d)
    tr = min(512, t)
    rw = jnp.pad(router_w, ((0, 0), (0, LANES - n_e)))
    rb = jnp.pad(router_b, (0, LANES - n_e)).reshape(1, LANES)
    idx, gates = pl.pallas_call(
        functools.partial(_router_body, n_experts=n_e),
        grid=(t // tr,),
        in_specs=[pl.BlockSpec((tr, d), lambda i: (i, 0)), pl.BlockSpec((1, d), lambda i: (0, 0)),
                  pl.BlockSpec((d, LANES), lambda i: (0, 0)), pl.BlockSpec((1, LANES), lambda i: (0, 0))],
        out_specs=[pl.BlockSpec((tr, LANES), lambda i: (i, 0)), pl.BlockSpec((tr, LANES), lambda i: (i, 0))],
        out_shape=[jax.ShapeDtypeStruct((t, LANES), jnp.int32), jax.ShapeDtypeStruct((t, LANES), F32)],
        compiler_params=_params(("parallel",)),
        name="moe_router",
    )(h, nw, rw, rb)
    tb = min(512, t)
    rank, cnt = pl.pallas_call(
        _rank_body,
        grid=(t // tb,),
        in_specs=[pl.BlockSpec((tb, LANES), lambda i: (i, 0))],
        out_specs=[pl.BlockSpec((tb, LANES), lambda i: (i, 0)), pl.BlockSpec((SUBLANES, LANES), lambda i: (0, 0))],
        out_shape=[jax.ShapeDtypeStruct((t, LANES), jnp.int32), jax.ShapeDtypeStruct((SUBLANES, LANES), F32)],
        scratch_shapes=[pltpu.VMEM((SUBLANES, LANES), F32)],
        compiler_params=_params(("arbitrary",)),
        name="moe_rank",
    )(idx)
    counts = cnt[0, :n_e].astype(jnp.int32)
    n_blocks = (t * TOP_K) // MOE_TM + n_e
    blocks_e = (counts + MOE_TM - 1) // MOE_TM
    blk_end = jnp.cumsum(blocks_e)
    blk_start = blk_end - blocks_e
    n_used = blk_end[-1]
    bid = jnp.arange(n_blocks, dtype=jnp.int32)
    blk_expert = jnp.minimum(jnp.searchsorted(blk_end, bid, side='right'), n_e - 1).astype(jnp.int32)
    last_used_expert = blk_expert[jnp.maximum(n_used - 1, 0)]
    blk_expert = jnp.where(bid < n_used, blk_expert, last_used_expert)
    rows_in_blk = jnp.clip(counts[blk_expert] - (bid - blk_start[blk_expert]) * MOE_TM, 0, MOE_TM)
    n_sub = jnp.where(bid < n_used, (rows_in_blk + MOE_SUB - 1) // MOE_SUB, 0).astype(jnp.int32)
    dest = blk_start[idx[:, :TOP_K]] * MOE_TM + rank[:, :TOP_K]
    tt = min(256, t)
    dest3 = dest.reshape(t // tt, 1, tt * TOP_K)
    n_rows = n_blocks * MOE_TM
    dp = d // 2
    xs = pl.pallas_call(
        functools.partial(_dispatch_body, n_steps=t // tt),
        grid=(t // tt,),
        in_specs=[pl.BlockSpec((1, 1, tt * TOP_K), lambda i: (i, 0, 0), memory_space=pltpu.SMEM),
                  pl.BlockSpec(memory_space=pltpu.SMEM),
                  pl.BlockSpec((tt, d), lambda i: (i, 0)), pl.BlockSpec((1, d), lambda i: (0, 0))],
        out_specs=pl.BlockSpec(memory_space=pl.ANY),
        out_shape=jax.ShapeDtypeStruct((n_rows, dp), jnp.uint32),
        scratch_shapes=[pltpu.VMEM((2, tt, dp), jnp.uint32), pltpu.VMEM((MOE_SUB, dp), jnp.uint32),
                        pltpu.SemaphoreType.DMA((2,)), pltpu.SemaphoreType.DMA],
        compiler_params=_params(("arbitrary",)),
        name="moe_dispatch",
    )(dest3, jnp.where(bid < n_used, rows_in_blk, 0).astype(jnp.int32), h, nw)
    nf = d_ff // MOE_TF
    last_f = nf - 1
    used = lambda b, f, nu: jnp.where(b < nu[0], f, last_f)
    ys = pl.pallas_call(
        _expert_body,
        grid_spec=pltpu.PrefetchScalarGridSpec(
            num_scalar_prefetch=3,
            grid=(n_blocks, nf),
            in_specs=[
                pl.BlockSpec((MOE_TM, dp), lambda b, f, be, ns, nu: (jnp.minimum(b, jnp.maximum(nu[0] - 1, 0)), 0)),
                pl.BlockSpec((1, 1, d, MOE_TF), lambda b, f, be, ns, nu: (layer, be[b], 0, used(b, f, nu))),
                pl.BlockSpec((1, 1, d, MOE_TF), lambda b, f, be, ns, nu: (layer, be[b], 0, nf + used(b, f, nu))),
                pl.BlockSpec((1, 1, MOE_TF, d), lambda b, f, be, ns, nu: (layer, be[b], used(b, f, nu), 0)),
                pl.BlockSpec((1, 1, 1, MOE_TF), lambda b, f, be, ns, nu: (layer, be[b], 0, used(b, f, nu))),
                pl.BlockSpec((1, 1, 1, MOE_TF), lambda b, f, be, ns, nu: (layer, be[b], 0, nf + used(b, f, nu))),
                pl.BlockSpec((1, 1, 1, d), lambda b, f, be, ns, nu: (layer, be[b], 0, 0)),
            ],
            out_specs=pl.BlockSpec((MOE_TM, dp), lambda b, f, be, ns, nu: (b, 0)),
            scratch_shapes=[pltpu.VMEM((MOE_TM, d), F32)],
        ),
        out_shape=jax.ShapeDtypeStruct((n_rows, dp), jnp.uint32),
        compiler_params=_params(("arbitrary", "arbitrary")),
        name="moe_experts",
    )(blk_expert, n_sub, n_used.reshape(1).astype(jnp.int32), xs, w_gate_up, w_gate_up, w_down,
      b_gate_up.reshape(-1, n_e, 1, 2 * d_ff), b_gate_up.reshape(-1, n_e, 1, 2 * d_ff), b_down.reshape(-1, n_e, 1, d))
    tc = min(256, t)
    destc = dest.reshape(t // tc, 1, tc * TOP_K)
    row_spec = pl.BlockSpec((tc, d), lambda i: (i, 0))
    out_specs, out_shape = row_spec, jax.ShapeDtypeStruct((t, d), F32)
    if not final:
        out_specs, out_shape = [row_spec, row_spec], [out_shape, jax.ShapeDtypeStruct((t, d), BF16)]
    n_comb = t // tc
    return pl.pallas_call(
        functools.partial(_combine_body, final=final, n_steps=n_comb),
        grid=(n_comb,),
        in_specs=[pl.BlockSpec((1, 1, tc * TOP_K), lambda i: (i, 0, 0), memory_space=pltpu.SMEM),
                  pl.BlockSpec((1, 1, tc * TOP_K), lambda i: (jnp.minimum(i + 1, n_comb - 1), 0, 0),
                               memory_space=pltpu.SMEM),
                  row_spec, pl.BlockSpec((tc, LANES), lambda i: (i, 0)), pl.BlockSpec((1, d), lambda i: (0, 0)),
                  pl.BlockSpec(memory_space=pl.ANY)],
        out_specs=out_specs,
        out_shape=out_shape,
        scratch_shapes=[pltpu.VMEM((2, TOP_K * tc, dp), jnp.uint32), pltpu.SemaphoreType.DMA((2,))],
        compiler_params=_params(("arbitrary",)),
        name="moe_combine",
    )(destc, destc, h, gates, next_norm_w.reshape(1, d), ys)


def kernel(x, norm_mix_w, norm_ffn_w, final_norm_w, a_in_proj, a_conv_w, a_conv_b, a_dt_bias, a_log, a_d_skip, a_norm_w, a_out_proj, b_in_proj, b_lam_re, b_lam_im, b_log_step, b_b_re, b_b_im, b_c_re, b_c_im, b_d_skip, b_glu_proj, router_w, router_b, w_gate_up, b_gate_up, w_down, b_down):
    bsz, seq, d = x.shape
    depth = norm_mix_w.shape[0]
    h = x.reshape(bsz * seq, d)
    xn = _rmsnorm(h, norm_mix_w[0], BF16)
    for i in range(depth):
        j = i // 2
        if i % 2 == 0:
            h = _ssd_mixer(h, xn, a_in_proj[j], a_conv_w[j], a_conv_b[j], a_dt_bias[j], a_log[j], a_d_skip[j],
                           a_norm_w[j], a_out_proj[j], bsz, seq)
        else:
            h = _s5_mixer(h, xn, b_in_proj[j], b_lam_re[j], b_lam_im[j], b_log_step[j], b_b_re[j], b_b_im[j],
                          b_c_re[j], b_c_im[j], b_d_skip[j], b_glu_proj[j], bsz, seq)
        if i == depth - 1:
            out = _moe(h, norm_ffn_w[i], router_w[i], router_b[i], w_gate_up, b_gate_up, w_down, b_down, i,
                       final_norm_w, True)
            return out.reshape(bsz, seq, d)
        h, xn = _moe(h, norm_ffn_w[i], router_w[i], router_b[i], w_gate_up, b_gate_up, w_down, b_down, i,
                     norm_mix_w[i + 1], False)
```

```python
import functools
import math

import jax
import jax.numpy as jnp
from jax import lax
from jax.experimental import pallas as pl
from jax.experimental.pallas import tpu as pltpu

F32 = jnp.float32
BF16 = jnp.bfloat16
HIGHEST = lax.Precision.HIGHEST

RMS_EPS = 1e-6
LOG2_E = 1.0 / math.log(2.0)
LANES = 128
SUBLANES = 8
V7X_VMEM_LIMIT_BYTES = 58 * 1024 * 1024

SSD_HEAD_DIM = 64
SSD_GROUPS = 8
SSD_HEADS_PER_GROUP = 8
SSD_STATE = 128
SSD_CONV = 4
SSD_CHUNK = 256
SSD_GROUP_W = SSD_HEADS_PER_GROUP * SSD_HEAD_DIM
SSD_PAIRS = SSD_GROUP_W // LANES

S5_GROUP_CH = 16
S5_STATE = 64
S5_CHUNK = 16
S5_OCT = LANES // S5_GROUP_CH

TOP_K = 4
SWIGLU_LIMIT = 7.0
SWIGLU_ALPHA = 1.702
MOE_TM = 1280
MOE_SUB = 256
MOE_TF = 256


def _params(semantics):
    return pltpu.CompilerParams(dimension_semantics=semantics, vmem_limit_bytes=V7X_VMEM_LIMIT_BYTES)


def _sigmoid(x):
    return 0.5 + 0.5 * jnp.tanh(0.5 * x)


def _silu(x):
    hx = 0.5 * x
    return hx + hx * jnp.tanh(hx)


def _pack_bf16_pair(lo, hi):
    ulo = lax.bitcast_convert_type(lo.astype(BF16).astype(F32), jnp.uint32)
    uhi = lax.bitcast_convert_type(hi.astype(BF16).astype(F32), jnp.uint32)
    return (ulo >> 16) | uhi


def _unpack_bf16_pair(w):
    lo = lax.bitcast_convert_type(w << 16, F32)
    hi = lax.bitcast_convert_type(w & jnp.uint32(0xFFFF0000), F32)
    return lo, hi


def _rms(x, w):
    ms = jnp.mean(x * x, axis=-1, keepdims=True)
    return x * lax.rsqrt(ms + RMS_EPS) * w


def _rmsnorm_body(x_ref, w_ref, o_ref):
    o_ref[...] = _rms(x_ref[...], w_ref[...]).astype(o_ref.dtype)


def _rmsnorm(x, w, out_dtype, tm=512):
    t, d = x.shape
    tm = min(tm, t)
    return pl.pallas_call(
        _rmsnorm_body,
        grid=(t // tm,),
        in_specs=[pl.BlockSpec((tm, d), lambda i: (i, 0)), pl.BlockSpec((1, d), lambda i: (0, 0))],
        out_specs=pl.BlockSpec((tm, d), lambda i: (i, 0)),
        out_shape=jax.ShapeDtypeStruct((t, d), out_dtype),
        compiler_params=_params(("parallel",)),
        name="rmsnorm",
    )(x, w.reshape(1, d))


def _mm_body(x_ref, w_ref, o_ref):
    o_ref[...] = jnp.dot(x_ref[...], w_ref[...], preferred_element_type=F32).astype(o_ref.dtype)


def _mm_res_body(x_ref, w_ref, r_ref, o_ref):
    o_ref[...] = r_ref[...] + jnp.dot(x_ref[...], w_ref[...], preferred_element_type=F32)


def _mm_glu_body(x_ref, wv_ref, wg_ref, r_ref, o_ref):
    x = x_ref[...].astype(BF16)
    v = jnp.dot(x, wv_ref[...], preferred_element_type=F32)
    g = jnp.dot(x, wg_ref[...], preferred_element_type=F32)
    o_ref[...] = r_ref[...] + v * _sigmoid(g)


def _matmul(x, w, out_dtype, tm, tn, name):
    m, k = x.shape
    n = w.shape[1]
    tm, tn = min(tm, m), min(tn, n)
    return pl.pallas_call(
        _mm_body,
        grid=(m // tm, n // tn),
        in_specs=[pl.BlockSpec((tm, k), lambda i, j: (i, 0)), pl.BlockSpec((k, tn), lambda i, j: (0, j))],
        out_specs=pl.BlockSpec((tm, tn), lambda i, j: (i, j)),
        out_shape=jax.ShapeDtypeStruct((m, n), out_dtype),
        compiler_params=_params(("parallel", "parallel")),
        name=name,
    )(x, w)


def _matmul_residual(x, w, res, tm, tn, name):
    m, k = x.shape
    n = w.shape[1]
    tm, tn = min(tm, m), min(tn, n)
    return pl.pallas_call(
        _mm_res_body,
        grid=(m // tm, n // tn),
        in_specs=[pl.BlockSpec((tm, k), lambda i, j: (i, 0)), pl.BlockSpec((k, tn), lambda i, j: (0, j)),
                  pl.BlockSpec((tm, tn), lambda i, j: (i, j))],
        out_specs=pl.BlockSpec((tm, tn), lambda i, j: (i, j)),
        out_shape=jax.ShapeDtypeStruct((m, n), F32),
        compiler_params=_params(("parallel", "parallel")),
        name=name,
    )(x, w, res)


def _matmul_glu_residual(x, w, res, tm, tn, name):
    m, k = x.shape
    n = w.shape[1] // 2
    tm, tn = min(tm, m), min(tn, n)
    nj = n // tn
    return pl.pallas_call(
        _mm_glu_body,
        grid=(m // tm, nj),
        in_specs=[pl.BlockSpec((tm, k), lambda i, j: (i, 0)),
                  pl.BlockSpec((k, tn), lambda i, j: (0, j)),
                  pl.BlockSpec((k, tn), lambda i, j: (0, nj + j)),
                  pl.BlockSpec((tm, tn), lambda i, j: (i, j))],
        out_specs=pl.BlockSpec((tm, tn), lambda i, j: (i, j)),
        out_shape=jax.ShapeDtypeStruct((m, n), F32),
        compiler_params=_params(("parallel", "parallel")),
        name=name,
    )(x, w, w, res)


def _ssd_body(z_ref, x_ref, b_ref, c_ref, dt_ref,
              cwx_ref, cwb_ref, cwc_ref, cbx_ref, cbb_ref, cbc_ref,
              dtb_ref, alog_ref, dexp_ref, nw_ref,
              o_ref,
              tx_ref, tb_ref, tc_ref, st_ref, pad_ref):
    c = pl.program_id(1)
    g = pl.program_id(2)
    lc = x_ref.shape[1]

    @pl.when(c == 0)
    def _():
        tx_ref[g] = jnp.zeros(tx_ref.shape[1:], F32)
        tb_ref[g] = jnp.zeros(tb_ref.shape[1:], F32)
        tc_ref[g] = jnp.zeros(tc_ref.shape[1:], F32)
        st_ref[g] = jnp.zeros(st_ref.shape[1:], F32)

    def conv_silu(u_ref, w_ref, bias_ref, tail_ref):
        width = u_ref.shape[2]
        u = u_ref[0].astype(F32)
        pad_ref[0:SUBLANES, 0:width] = tail_ref[g]
        pad_ref[SUBLANES:2 * SUBLANES, 0:width] = u[0:SUBLANES, :]
        tail_ref[g] = u[lc - SUBLANES:lc, :]
        acc = bias_ref[...] + w_ref[SSD_CONV - 1:SSD_CONV, :] * u
        for k in range(SSD_CONV - 1):
            sh = SSD_CONV - 1 - k
            body = pltpu.roll(u, sh, 0)
            head = pad_ref[SUBLANES - sh:2 * SUBLANES - sh, 0:width]
            acc = acc + w_ref[k:k + 1, :] * jnp.concatenate([head, body[SUBLANES:, :]], axis=0)
        return _silu(acc)

    xg = conv_silu(x_ref, cwx_ref, cbx_ref, tx_ref)
    bg = conv_silu(b_ref, cwb_ref, cbb_ref, tb_ref)
    cg = conv_silu(c_ref, cwc_ref, cbc_ref, tc_ref)

    dt_in = dt_ref[0] + dtb_ref[0]
    dt = jnp.maximum(dt_in, 0.0) + jnp.log1p(jnp.exp(-jnp.abs(dt_in)))
    a = dt * (-jnp.exp(alog_ref[0]))
    rows = lax.broadcasted_iota(jnp.int32, (lc, lc), 0)
    cols = lax.broadcasted_iota(jnp.int32, (lc, lc), 1)
    a_cum = jnp.dot((rows >= cols).astype(F32), a, precision=HIGHEST, preferred_element_type=F32)
    a2 = a_cum * LOG2_E
    a2_t = a2.T
    src_t = a2_t - jnp.log(dt.T) * LOG2_E

    bgt = bg.T
    cb = jnp.dot(cg.astype(BF16), bgt.astype(BF16), preferred_element_type=F32)
    left = lax.broadcasted_iota(jnp.int32, (1, LANES), 1) < SSD_HEAD_DIM
    n_blk = lc // LANES
    diag_mask = (lax.broadcasted_iota(jnp.int32, (LANES, LANES), 0) >= lax.broadcasted_iota(jnp.int32, (LANES, LANES), 1))
    neg_inf = jnp.float32(-jnp.inf)

    ys = []
    for p in range(SSD_PAIRS):
        xp = xg[:, p * LANES:(p + 1) * LANES]
        xpb = xp.astype(BF16)
        st = st_ref[g, p]
        stb = st.astype(BF16)
        y_h, st_h = [], []
        for hh in range(2):
            r = 2 * p + hh
            colb = jnp.broadcast_to(a2[:, r:r + 1], (lc, LANES))
            srow = src_t[r:r + 1, :]
            a2_tot = a2_t[r:r + 1, lc - 1:lc]
            cs = (cg * jnp.exp2(colb)).astype(BF16)
            y_rows = []
            for i in range(n_blk):
                ci = colb[i * LANES:(i + 1) * LANES]
                parts = []
                for j in range(i + 1):
                    seg = ci - srow[:, j * LANES:(j + 1) * LANES]
                    if j == i:
                        seg = jnp.where(diag_mask, seg, neg_inf)
                    parts.append((cb[i * LANES:(i + 1) * LANES, j * LANES:(j + 1) * LANES] * jnp.exp2(seg)).astype(BF16))
                lhs = jnp.concatenate(parts + [cs[i * LANES:(i + 1) * LANES]], axis=1)
                rhs = jnp.concatenate([xpb[:(i + 1) * LANES], stb], axis=0)
                y_rows.append(jnp.dot(lhs, rhs, preferred_element_type=F32))
            y_h.append(jnp.concatenate(y_rows, axis=0))
            w_row = jnp.exp2(a2_tot - srow)
            s_new = jnp.dot((bgt * w_row).astype(BF16), xpb, preferred_element_type=F32)
            st_h.append(st * jnp.exp2(a2_tot) + s_new)
        ys.append(jnp.where(left, y_h[0], y_h[1]) + dexp_ref[:, p * LANES:(p + 1) * LANES] * xp)
        st_ref[g, p] = jnp.where(left, st_h[0], st_h[1])

    y = jnp.concatenate(ys, axis=1)
    yz = y * _silu(z_ref[0].astype(F32))
    ms = jnp.mean(yz * yz, axis=-1, keepdims=True)
    o_ref[0] = (yz * lax.rsqrt(ms + RMS_EPS) * nw_ref[...]).astype(o_ref.dtype)


def _ssd(zx, dt, conv_w, conv_b, dt_bias, a_log, d_skip, norm_w):
    bsz, seq, _ = zx.shape
    g_n, gw, n = SSD_GROUPS, SSD_GROUP_W, SSD_STATE
    d_inner = g_n * gw
    lc = min(SSD_CHUNK, seq)
    nc = seq // lc
    xb0 = d_inner // gw
    bb0 = 2 * d_inner // n
    cb0 = bb0 + g_n
    cwx, cwb, cwc = conv_w[:, :d_inner], conv_w[:, d_inner:d_inner + g_n * n], conv_w[:, d_inner + g_n * n:]
    cb_ = conv_b.reshape(1, -1)
    cbx, cbb, cbc = cb_[:, :d_inner], cb_[:, d_inner:d_inner + g_n * n], cb_[:, d_inner + g_n * n:]
    pad_heads = lambda v: jnp.pad(v.reshape(g_n, 1, SSD_HEADS_PER_GROUP), ((0, 0), (0, 0), (0, LANES - SSD_HEADS_PER_GROUP)))
    dexp = jnp.repeat(d_skip, SSD_HEAD_DIM).reshape(1, d_inner)
    grp = lambda w: pl.BlockSpec((1, lc, w), lambda b, c, g: (b, c, g))
    return pl.pallas_call(
        _ssd_body,
        grid=(bsz, nc, g_n),
        in_specs=[
            pl.BlockSpec((1, lc, gw), lambda b, c, g: (b, c, g)),
            pl.BlockSpec((1, lc, gw), lambda b, c, g: (b, c, xb0 + g)),
            pl.BlockSpec((1, lc, n), lambda b, c, g: (b, c, bb0 + g)),
            pl.BlockSpec((1, lc, n), lambda b, c, g: (b, c, cb0 + g)),
            pl.BlockSpec((1, lc, LANES), lambda b, c, g: (b, c, g)),
            pl.BlockSpec((SSD_CONV, gw), lambda b, c, g: (0, g)),
            pl.BlockSpec((SSD_CONV, n), lambda b, c, g: (0, g)),
            pl.BlockSpec((SSD_CONV, n), lambda b, c, g: (0, g)),
            pl.BlockSpec((1, gw), lambda b, c, g: (0, g)),
            pl.BlockSpec((1, n), lambda b, c, g: (0, g)),
            pl.BlockSpec((1, n), lambda b, c, g: (0, g)),
            pl.BlockSpec((1, 1, LANES), lambda b, c, g: (g, 0, 0)),
            pl.BlockSpec((1, 1, LANES), lambda b, c, g: (g, 0, 0)),
            pl.BlockSpec((1, gw), lambda b, c, g: (0, g)),
            pl.BlockSpec((1, gw), lambda b, c, g: (0, g)),
        ],
        out_specs=pl.BlockSpec((1, lc, gw), lambda b, c, g: (b, c, g)),
        out_shape=jax.ShapeDtypeStruct((bsz, seq, d_inner), BF16),
        scratch_shapes=[
            pltpu.VMEM((g_n, SUBLANES, gw), F32),
            pltpu.VMEM((g_n, SUBLANES, n), F32),
            pltpu.VMEM((g_n, SUBLANES, n), F32),
            pltpu.VMEM((g_n, SSD_PAIRS, n, LANES), F32),
            pltpu.VMEM((2 * SUBLANES, gw), F32),
        ],
        compiler_params=_params(("arbitrary", "arbitrary", "arbitrary")),
        name="ssd_scan",
    )(zx, zx, zx, zx, dt, cwx, cwb, cwc, cbx, cbb, cbc, pad_heads(dt_bias), pad_heads(a_log), dexp,
      norm_w.reshape(1, d_inner))


def _ssd_mixer(h, xn, in_proj, conv_w, conv_b, dt_bias, a_log, d_skip, norm_w, out_proj, bsz, seq):
    t, d = h.shape
    n_heads = dt_bias.shape[0]
    d_main = in_proj.shape[1] - n_heads
    w_main = in_proj[:, :d_main].astype(BF16)
    w_dt = jnp.pad(in_proj[:, d_main:], ((0, 0), (0, LANES - n_heads))).astype(BF16)
    zx = _matmul(xn, w_main, BF16, 1024, 1024, "ssd_in_proj")
    dt_raw = _matmul(xn, w_dt, F32, 1024, LANES, "ssd_dt_proj")
    dt = dt_raw[:, :n_heads].reshape(t, SSD_GROUPS, SSD_HEADS_PER_GROUP)
    dt = jnp.pad(dt, ((0, 0), (0, 0), (0, LANES - SSD_HEADS_PER_GROUP))).reshape(bsz, seq, SSD_GROUPS * LANES)
    y = _ssd(zx.reshape(bsz, seq, d_main), dt, conv_w, conv_b, dt_bias, a_log, d_skip, norm_w)
    return _matmul_residual(y.reshape(t, -1), out_proj.astype(BF16), h, 512, 1024, "ssd_out_proj")


def _s5_body(u_ref, d_ref, win_ref, wout_ref, are_ref, aim_ref, o_ref, *, n_chunks):
    r = u_ref.shape[0] // S5_CHUNK
    pw = 2 * LANES
    n_pairs = S5_CHUNK // 2
    n_state = are_ref.shape[2]
    z = jnp.concatenate([u_ref[pl.ds(s, r, stride=S5_CHUNK), :] for s in range(S5_CHUNK)], axis=1).astype(BF16)
    xin = jnp.dot(z, win_ref[0], preferred_element_type=F32)
    xre, xim = xin[:, :n_state], xin[:, n_state:]
    kidx = lax.broadcasted_iota(jnp.int32, (r, n_state), 0) % n_chunks

    def shifted(v, sh):
        return jnp.where(kidx >= sh, pltpu.roll(v, sh, 0), 0.0)

    sh, lvl = 1, 0
    while sh < n_chunks:
        are, aim = are_ref[0, lvl:lvl + 1, :], aim_ref[0, lvl:lvl + 1, :]
        sre, sim = shifted(xre, sh), shifted(xim, sh)
        xre, xim = xre + are * sre - aim * sim, xim + are * sim + aim * sre
        sh, lvl = sh * 2, lvl + 1
    prev = jnp.concatenate([shifted(xre, 1), shifted(xim, 1)], axis=1).astype(BF16)
    carry = jnp.dot(prev, wout_ref[0], preferred_element_type=F32)
    for tp in range(n_pairs):
        y2 = jnp.dot(z[:, :(tp + 1) * pw], d_ref[0, (n_pairs - 1 - tp) * pw:, :], preferred_element_type=F32)
        y2 = y2 + carry[:, tp * pw:(tp + 1) * pw]
        gelu = 0.5 * y2 * (1.0 + jnp.tanh(math.sqrt(2.0 / math.pi) * (y2 + 0.044715 * (y2 * y2 * y2))))
        for i in range(2):
            o_ref[pl.ds(2 * tp + i, r, stride=S5_CHUNK), :] = gelu[:, i * LANES:(i + 1) * LANES]


def _s5_tables(lam_re, lam_im, log_step, b_re, b_im, c_re, c_im, d_skip, n_chunks):
    g_n, p_n = lam_re.shape
    ch, lcs, oct_ = S5_GROUP_CH, S5_CHUNK, S5_OCT
    n_oct = g_n // oct_
    lam = lax.complex(lam_re, lam_im)
    step = jnp.exp(log_step)[:, None]
    lam_bar = jnp.exp(lam * step)
    b_bar = ((lam_bar - 1.0) / lam)[..., None] * lax.complex(b_re, b_im)
    cc = lax.complex(c_re, c_im)
    taus = jnp.arange(lcs + 1, dtype=F32)
    lam_pow = jnp.exp((lam * step)[None] * taus[:, None, None])

    def spread(w, src_lane, grp_lane, g_axis):
        sel = (jnp.arange(w.shape[-1])[:, None] == src_lane[None, :]).astype(BF16)
        v = jnp.dot(w.astype(BF16), sel, preferred_element_type=F32)
        gshape = [1] * v.ndim
        gshape[g_axis] = oct_
        return jnp.where(jnp.arange(oct_).reshape(gshape) == grp_lane, v, 0.0).astype(BF16)

    kern = jnp.einsum('gcp,tgp,gpd->gtcd', cc, lam_pow[:lcs], b_bar, precision=HIGHEST).real
    kern = kern.at[:, 0].add(d_skip.reshape(g_n, ch)[:, :, None] * jnp.eye(ch, dtype=F32))
    kz = jnp.concatenate([jnp.zeros((g_n, 1, ch, ch), F32), kern], axis=1)
    dl = jnp.arange(lcs // 2)[:, None, None]
    s2 = jnp.arange(2)[None, :, None]
    t2 = jnp.arange(2)[None, None, :]
    blk = kz[:, 2 * dl + t2 - s2 + 1]
    blk = jnp.transpose(blk.reshape(n_oct, oct_, lcs // 2, 2, 2, ch, ch), (0, 2, 3, 1, 6, 4, 5))
    lane2 = jnp.arange(2 * LANES)
    dtab = spread(blk.reshape(n_oct, lcs // 2, 2, oct_, ch, 2 * ch),
                  (lane2 // LANES) * ch + lane2 % ch, (lane2 % LANES) // ch, 3)
    dtab = dtab.reshape(n_oct, lcs // 2, 2 * LANES, 2 * LANES)[:, ::-1].reshape(n_oct, lcs * LANES, 2 * LANES)
    win = jnp.einsum('sgp,gpd->gsdp', lam_pow[:lcs][::-1], b_bar)
    win = jnp.transpose(win.reshape(n_oct, oct_, lcs, ch, p_n), (0, 2, 1, 3, 4))
    n_state = oct_ * p_n
    lane_s = jnp.arange(2 * n_state)
    win_t = spread(jnp.concatenate([win.real, win.imag], axis=-1),
                   (lane_s // n_state) * p_n + lane_s % p_n, (lane_s % n_state) // p_n, 2)
    win_t = win_t.reshape(n_oct, lcs * LANES, 2 * n_state)
    wout = jnp.einsum('gcp,tgp->gptc', cc, lam_pow[1:lcs + 1]).reshape(n_oct, oct_, p_n, lcs * ch)
    lane_o = jnp.arange(lcs * LANES)
    src_o, grp_o = (lane_o // LANES) * ch + lane_o % ch, (lane_o % LANES) // ch
    wout_t = jnp.concatenate([spread(wout.real, src_o, grp_o, 1).reshape(n_oct, n_state, lcs * LANES),
                              spread(-wout.imag, src_o, grp_o, 1).reshape(n_oct, n_state, lcs * LANES)], axis=1)
    n_lvl = max(1, int(math.ceil(math.log2(max(n_chunks, 2)))))
    lvl_pow = lcs * (2.0 ** jnp.arange(n_lvl, dtype=F32))
    a_pow = jnp.exp((lam * step)[None] * lvl_pow[:, None, None])
    a_pow = jnp.transpose(a_pow.reshape(n_lvl, n_oct, oct_ * p_n), (1, 0, 2))
    a_pad = ((0, 0), (0, -n_lvl % SUBLANES), (0, 0))
    return (dtab, win_t, wout_t,
            jnp.pad(a_pow.real, a_pad), jnp.pad(a_pow.imag, a_pad))


def _s5_mixer(h, xn, in_proj, lam_re, lam_im, log_step, b_re, b_im, c_re, c_im, d_skip, glu_proj, bsz, seq):
    t, d = h.shape
    g_n, p_n = lam_re.shape
    width = g_n * S5_GROUP_CH
    n_chunks = seq // S5_CHUNK
    n_oct = g_n // S5_OCT
    u = _matmul(xn, in_proj.astype(BF16), F32, 1024, 1024, "s5_in_proj")
    dtab, win_t, wout_t, are, aim = _s5_tables(lam_re, lam_im, log_step, b_re, b_im, c_re, c_im, d_skip, n_chunks)
    n_split = 2 if bsz % 2 == 0 else 1
    rows = t // n_split
    tab = lambda a: pl.BlockSpec((1,) + a.shape[1:], lambda j, i: (j, 0, 0))
    gact = pl.pallas_call(
        functools.partial(_s5_body, n_chunks=n_chunks),
        grid=(n_oct, n_split),
        in_specs=[pl.BlockSpec((rows, LANES), lambda j, i: (i, j)), tab(dtab), tab(win_t), tab(wout_t), tab(are), tab(aim)],
        out_specs=pl.BlockSpec((rows, LANES), lambda j, i: (i, j)),
        out_shape=jax.ShapeDtypeStruct((t, width), F32),
        compiler_params=_params(("parallel", "parallel")),
        name="s5_scan",
    )(u, dtab, win_t, wout_t, are, aim)
    return _matmul_glu_residual(gact, glu_proj.astype(BF16), h, 512, 1024, "s5_glu_proj")


def _router_body(h_ref, nw_ref, rw_ref, rb_ref, idx_ref, gate_ref, *, n_experts):
    xn = _rms(h_ref[...], nw_ref[...])
    xh = xn.astype(BF16)
    xm = (xn - xh.astype(F32)).astype(BF16)
    logits = jnp.dot(jnp.concatenate([xh, xh, xm], axis=1), rw_ref[...], preferred_element_type=F32) + rb_ref[...]
    tm = logits.shape[0]
    lane = lax.broadcasted_iota(jnp.int32, (tm, LANES), 1)
    lane_f = lane.astype(F32)
    neg_inf = jnp.float32(-jnp.inf)
    cur = jnp.where(lane < n_experts, logits, neg_inf)
    tops, idxs = [], []
    for _ in range(TOP_K):
        mx = jnp.max(cur, axis=-1, keepdims=True)
        ix = jnp.min(jnp.where(cur == mx, lane_f, float(LANES)), axis=-1, keepdims=True)
        cur = jnp.where(lane_f == ix, neg_inf, cur)
        tops.append(mx)
        idxs.append(ix)
    es = [jnp.exp(v - tops[0]) for v in tops]
    denom = es[0] + es[1] + es[2] + es[3]
    idx_out = jnp.zeros((tm, LANES), F32)
    gate_out = jnp.zeros((tm, LANES), F32)
    for k in range(TOP_K):
        idx_out = jnp.where(lane == k, idxs[k], idx_out)
        gate_out = jnp.where(lane == k, es[k] / denom, gate_out)
    idx_ref[...] = idx_out.astype(jnp.int32)
    gate_ref[...] = gate_out


def _rank_body(idx_ref, rank_ref, cnt_ref, carry_ref):
    i = pl.program_id(0)
    tb = idx_ref.shape[0]

    @pl.when(i == 0)
    def _():
        carry_ref[...] = jnp.zeros(carry_ref.shape, F32)

    idx = idx_ref[...]
    lane = lax.broadcasted_iota(jnp.int32, (tb, LANES), 1)
    onehots = [(lane == idx[:, k:k + 1]) for k in range(TOP_K)]
    hits = jnp.zeros((tb, LANES), F32)
    for oh in onehots:
        hits = hits + oh.astype(F32)
    rows = lax.broadcasted_iota(jnp.int32, (tb, tb), 0)
    cols = lax.broadcasted_iota(jnp.int32, (tb, tb), 1)
    strict = (rows > cols).astype(BF16)
    before = jnp.dot(strict, hits.astype(BF16), preferred_element_type=F32) + carry_ref[0:1, :]
    out = jnp.zeros((tb, LANES), jnp.int32)
    for k, oh in enumerate(onehots):
        rk = jnp.sum(jnp.where(oh, before, 0.0), axis=-1, keepdims=True)
        out = jnp.where(lane == k, rk.astype(jnp.int32), out)
    rank_ref[...] = out
    total = carry_ref[0:1, :] + jnp.sum(hits, axis=0, keepdims=True)
    carry_ref[...] = jnp.broadcast_to(total, carry_ref.shape)
    cnt_ref[...] = jnp.broadcast_to(total, cnt_ref.shape)


def _dispatch_body(dest_ref, rib_ref, h_ref, nw_ref, xs_ref, buf_ref, zero_ref, sems, zero_sem, *, n_steps):
    tt, d = h_ref.shape
    step = pl.program_id(0)
    slot = lax.rem(step, 2)

    @pl.when(step == 0)
    def _():
        zero_ref[...] = jnp.zeros(zero_ref.shape, zero_ref.dtype)

        def zero_copy(b, j):
            return pltpu.make_async_copy(zero_ref, xs_ref.at[pl.ds(b * MOE_TM + j * MOE_SUB, MOE_SUB), :], zero_sem)

        def zero_block(op):
            def body(b, carry):
                rows_in_block = rib_ref[b]
                for j in range(MOE_TM // MOE_SUB):
                    @pl.when(rows_in_block < (j + 1) * MOE_SUB)
                    def _():
                        op(zero_copy(b, j))
                return carry
            return body

        lax.fori_loop(0, rib_ref.shape[0], zero_block(lambda cp: cp.start()), 0)
        lax.fori_loop(0, rib_ref.shape[0], zero_block(lambda cp: cp.wait()), 0)

    def drain(s):
        for _ in range(TOP_K):
            pltpu.make_async_copy(buf_ref.at[s], xs_ref.at[pl.ds(0, tt), :], sems.at[s]).wait()

    @pl.when(step >= 2)
    def _():
        drain(slot)

    xn = _rms(h_ref[...], nw_ref[...])
    buf_ref[slot] = _pack_bf16_pair(xn[:, :d // 2], xn[:, d // 2:])

    for s in range(2):
        @pl.when(slot == s)
        def _():
            def start(i, carry):
                for k in range(TOP_K):
                    row = dest_ref[0, 0, i * TOP_K + k]
                    pltpu.make_async_copy(buf_ref.at[s, pl.ds(i, 1), :], xs_ref.at[pl.ds(row, 1), :], sems.at[s]).start()
                return carry

            lax.fori_loop(0, tt, start, 0, unroll=2)

    @pl.when(step == n_steps - 1)
    def _():
        if n_steps >= 2:
            drain(1 - slot)
        drain(slot)


def _combine_body(dest_ref, dest_next_ref, h_ref, gate_ref, nw_ref, ys_ref, *rest, final, n_steps):
    if final:
        o_ref, buf_ref, sems = rest
    else:
        o_ref, on_ref, buf_ref, sems = rest
    tt = h_ref.shape[0]
    step = pl.program_id(0)
    slot = lax.rem(step, 2)

    def issue(idx_ref, s):
        def start(i, carry):
            for k in range(TOP_K):
                row = idx_ref[0, 0, i * TOP_K + k]
                pltpu.make_async_copy(ys_ref.at[pl.ds(row, 1), :], buf_ref.at[s, pl.ds(k * tt + i, 1), :],
                                      sems.at[s]).start()
            return carry
        lax.fori_loop(0, tt, start, 0, unroll=2)

    @pl.when(step == 0)
    def _():
        issue(dest_ref, 0)

    for s in range(2):
        @pl.when((step + 1 < n_steps) & (slot == s))
        def _():
            issue(dest_next_ref, 1 - s)

    pltpu.make_async_copy(ys_ref.at[pl.ds(0, TOP_K * tt), :], buf_ref.at[slot], sems.at[slot]).wait()
    gates = gate_ref[...]
    h = h_ref[...]
    half = h.shape[1] // 2
    acc_lo, acc_hi = h[:, :half], h[:, half:]
    for k in range(TOP_K):
        lo, hi = _unpack_bf16_pair(buf_ref[slot, k * tt:(k + 1) * tt, :])
        acc_lo = acc_lo + gates[:, k:k + 1] * lo
        acc_hi = acc_hi + gates[:, k:k + 1] * hi
    acc = jnp.concatenate([acc_lo, acc_hi], axis=1)
    if final:
        o_ref[...] = _rms(acc, nw_ref[...])
    else:
        o_ref[...] = acc
        on_ref[...] = _rms(acc, nw_ref[...]).astype(on_ref.dtype)


def _expert_body(be_ref, ns_ref, nu_ref, x_ref, wg_ref, wu_ref, wd_ref, bg_ref, bu_ref, bd_ref, o_ref, acc_ref):
    del be_ref, nu_ref
    b = pl.program_id(0)
    f = pl.program_id(1)
    n_sub = ns_ref[b]
    d = acc_ref.shape[1]

    @pl.when(f == 0)
    def _():
        acc_ref[...] = jnp.broadcast_to(bd_ref[0, 0], acc_ref.shape)

    for ns in range(1, MOE_TM // MOE_SUB + 1):
        @pl.when(n_sub == ns)
        def _():
            rows = ns * MOE_SUB
            lo, hi = _unpack_bf16_pair(x_ref[0:rows, :])
            x = jnp.concatenate([lo.astype(BF16), hi.astype(BF16)], axis=1)
            gate = jnp.dot(x, wg_ref[0, 0].astype(BF16), preferred_element_type=F32) + bg_ref[0, 0]
            up = jnp.dot(x, wu_ref[0, 0].astype(BF16), preferred_element_type=F32) + bu_ref[0, 0]
            gate = jnp.minimum(gate, SWIGLU_LIMIT)
            up = jnp.clip(up, -SWIGLU_LIMIT, SWIGLU_LIMIT)
            act = ((up + 1.0) * (gate * _sigmoid(SWIGLU_ALPHA * gate))).astype(BF16)
            for half in range(2):
                cs = slice(half * d // 2, (half + 1) * d // 2)
                wd = wd_ref[0, 0, :, cs].astype(BF16)
                acc_ref[0:rows, cs] = acc_ref[0:rows, cs] + jnp.dot(act, wd, preferred_element_type=F32)

    @pl.when(f == pl.num_programs(1) - 1)
    def _():
        o_ref[...] = _pack_bf16_pair(acc_ref[:, :d // 2], acc_ref[:, d // 2:])


def _moe(h, norm_w, router_w, router_b, w_gate_up, b_gate_up, w_down, b_down, layer, next_norm_w, final):
    t, d = h.shape
    n_e = router_w.shape[1]
    d_ff = w_down.shape[2]
    nw = norm_w.reshape(1, d)
    tr = min(512, t)
    rw = jnp.pad(router_w, ((0, 0), (0, LANES - n_e)))
    rw_hi = rw.astype(BF16)
    rw_mid = (rw - rw_hi.astype(F32)).astype(BF16)
    rw = jnp.concatenate([rw_hi, rw_mid, rw_hi], axis=0)
    rb = jnp.pad(router_b, (0, LANES - n_e)).reshape(1, LANES)
    idx, gates = pl.pallas_call(
        functools.partial(_router_body, n_experts=n_e),
        grid=(t // tr,),
        in_specs=[pl.BlockSpec((tr, d), lambda i: (i, 0)), pl.BlockSpec((1, d), lambda i: (0, 0)),
                  pl.BlockSpec((3 * d, LANES), lambda i: (0, 0)), pl.BlockSpec((1, LANES), lambda i: (0, 0))],
        out_specs=[pl.BlockSpec((tr, LANES), lambda i: (i, 0)), pl.BlockSpec((tr, LANES), lambda i: (i, 0))],
        out_shape=[jax.ShapeDtypeStruct((t, LANES), jnp.int32), jax.ShapeDtypeStruct((t, LANES), F32)],
        compiler_params=_params(("parallel",)),
        name="moe_router",
    )(h, nw, rw, rb)
    tb = min(512, t)
    rank, cnt = pl.pallas_call(
        _rank_body,
        grid=(t // tb,),
        in_specs=[pl.BlockSpec((tb, LANES), lambda i: (i, 0))],
        out_specs=[pl.BlockSpec((tb, LANES), lambda i: (i, 0)), pl.BlockSpec((SUBLANES, LANES), lambda i: (0, 0))],
        out_shape=[jax.ShapeDtypeStruct((t, LANES), jnp.int32), jax.ShapeDtypeStruct((SUBLANES, LANES), F32)],
        scratch_shapes=[pltpu.VMEM((SUBLANES, LANES), F32)],
        compiler_params=_params(("arbitrary",)),
        name="moe_rank",
    )(idx)
    counts = cnt[0, :n_e].astype(jnp.int32)
    n_blocks = -(-t * TOP_K // MOE_TM) + n_e
    blocks_e = (counts + MOE_TM - 1) // MOE_TM
    blk_end = jnp.cumsum(blocks_e)
    blk_start = blk_end - blocks_e
    n_used = blk_end[-1]
    bid = jnp.arange(n_blocks, dtype=jnp.int32)
    blk_expert = jnp.minimum(jnp.searchsorted(blk_end, bid, side='right'), n_e - 1).astype(jnp.int32)
    last_used_expert = blk_expert[jnp.maximum(n_used - 1, 0)]
    blk_expert = jnp.where(bid < n_used, blk_expert, last_used_expert)
    rows_in_blk = jnp.clip(counts[blk_expert] - (bid - blk_start[blk_expert]) * MOE_TM, 0, MOE_TM)
    n_sub = jnp.where(bid < n_used, (rows_in_blk + MOE_SUB - 1) // MOE_SUB, 0).astype(jnp.int32)
    dest = blk_start[idx[:, :TOP_K]] * MOE_TM + rank[:, :TOP_K]
    tt = min(256, t)
    dest3 = dest.reshape(t // tt, 1, tt * TOP_K)
    n_rows = n_blocks * MOE_TM
    dp = d // 2
    xs = pl.pallas_call(
        functools.partial(_dispatch_body, n_steps=t // tt),
        grid=(t // tt,),
        in_specs=[pl.BlockSpec((1, 1, tt * TOP_K), lambda i: (i, 0, 0), memory_space=pltpu.SMEM),
                  pl.BlockSpec(memory_space=pltpu.SMEM),
                  pl.BlockSpec((tt, d), lambda i: (i, 0)), pl.BlockSpec((1, d), lambda i: (0, 0))],
        out_specs=pl.BlockSpec(memory_space=pl.ANY),
        out_shape=jax.ShapeDtypeStruct((n_rows, dp), jnp.uint32),
        scratch_shapes=[pltpu.VMEM((2, tt, dp), jnp.uint32), pltpu.VMEM((MOE_SUB, dp), jnp.uint32),
                        pltpu.SemaphoreType.DMA((2,)), pltpu.SemaphoreType.DMA],
        compiler_params=_params(("arbitrary",)),
        name="moe_dispatch",
    )(dest3, jnp.where(bid < n_used, rows_in_blk, 0).astype(jnp.int32), h, nw)
    nf = d_ff // MOE_TF
    last_f = nf - 1
    used = lambda b, f, nu: jnp.where(b < nu[0], f, last_f)
    ys = pl.pallas_call(
        _expert_body,
        grid_spec=pltpu.PrefetchScalarGridSpec(
            num_scalar_prefetch=3,
            grid=(n_blocks, nf),
            in_specs=[
                pl.BlockSpec((MOE_TM, dp), lambda b, f, be, ns, nu: (jnp.minimum(b, jnp.maximum(nu[0] - 1, 0)), 0)),
                pl.BlockSpec((1, 1, d, MOE_TF), lambda b, f, be, ns, nu: (layer, be[b], 0, used(b, f, nu))),
                pl.BlockSpec((1, 1, d, MOE_TF), lambda b, f, be, ns, nu: (layer, be[b], 0, nf + used(b, f, nu))),
                pl.BlockSpec((1, 1, MOE_TF, d), lambda b, f, be, ns, nu: (layer, be[b], used(b, f, nu), 0)),
                pl.BlockSpec((1, 1, 1, MOE_TF), lambda b, f, be, ns, nu: (layer, be[b], 0, used(b, f, nu))),
                pl.BlockSpec((1, 1, 1, MOE_TF), lambda b, f, be, ns, nu: (layer, be[b], 0, nf + used(b, f, nu))),
                pl.BlockSpec((1, 1, 1, d), lambda b, f, be, ns, nu: (layer, be[b], 0, 0)),
            ],
            out_specs=pl.BlockSpec((MOE_TM, dp), lambda b, f, be, ns, nu: (b, 0)),
            scratch_shapes=[pltpu.VMEM((MOE_TM, d), F32)],
        ),
        out_shape=jax.ShapeDtypeStruct((n_rows, dp), jnp.uint32),
        compiler_params=_params(("arbitrary", "arbitrary")),
        name="moe_experts",
    )(blk_expert, n_sub, n_used.reshape(1).astype(jnp.int32), xs, w_gate_up, w_gate_up, w_down,
      b_gate_up.reshape(-1, n_e, 1, 2 * d_ff), b_gate_up.reshape(-1, n_e, 1, 2 * d_ff), b_down.reshape(-1, n_e, 1, d))
    tc = min(256, t)
    destc = dest.reshape(t // tc, 1, tc * TOP_K)
    row_spec = pl.BlockSpec((tc, d), lambda i: (i, 0))
    out_specs, out_shape = row_spec, jax.ShapeDtypeStruct((t, d), F32)
    if not final:
        out_specs, out_shape = [row_spec, row_spec], [out_shape, jax.ShapeDtypeStruct((t, d), BF16)]
    n_comb = t // tc
    return pl.pallas_call(
        functools.partial(_combine_body, final=final, n_steps=n_comb),
        grid=(n_comb,),
        in_specs=[pl.BlockSpec((1, 1, tc * TOP_K), lambda i: (i, 0, 0), memory_space=pltpu.SMEM),
                  pl.BlockSpec((1, 1, tc * TOP_K), lambda i: (jnp.minimum(i + 1, n_comb - 1), 0, 0),
                               memory_space=pltpu.SMEM),
                  row_spec, pl.BlockSpec((tc, LANES), lambda i: (i, 0)), pl.BlockSpec((1, d), lambda i: (0, 0)),
                  pl.BlockSpec(memory_space=pl.ANY)],
        out_specs=out_specs,
        out_shape=out_shape,
        scratch_shapes=[pltpu.VMEM((2, TOP_K * tc, dp), jnp.uint32), pltpu.SemaphoreType.DMA((2,))],
        compiler_params=_params(("arbitrary",)),
        name="moe_combine",
    )(destc, destc, h, gates, next_norm_w.reshape(1, d), ys)


def kernel(x, norm_mix_w, norm_ffn_w, final_norm_w, a_in_proj, a_conv_w, a_conv_b, a_dt_bias, a_log, a_d_skip, a_norm_w, a_out_proj, b_in_proj, b_lam_re, b_lam_im, b_log_step, b_b_re, b_b_im, b_c_re, b_c_im, b_d_skip, b_glu_proj, router_w, router_b, w_gate_up, b_gate_up, w_down, b_down):
    bsz, seq, d = x.shape
    depth = norm_mix_w.shape[0]
    h = x.reshape(bsz * seq, d)
    xn = _rmsnorm(h, norm_mix_w[0], BF16)
    for i in range(depth):
        j = i // 2
        if i % 2 == 0:
            h = _ssd_mixer(h, xn, a_in_proj[j], a_conv_w[j], a_conv_b[j], a_dt_bias[j], a_log[j], a_d_skip[j],
                           a_norm_w[j], a_out_proj[j], bsz, seq)
        else:
            h = _s5_mixer(h, xn, b_in_proj[j], b_lam_re[j], b_lam_im[j], b_log_step[j], b_b_re[j], b_b_im[j],
                          b_c_re[j], b_c_im[j], b_d_skip[j], b_glu_proj[j], bsz, seq)
        if i == depth - 1:
            out = _moe(h, norm_ffn_w[i], router_w[i], router_b[i], w_gate_up, b_gate_up, w_down, b_down, i,
                       final_norm_w, True)
            return out.reshape(bsz, seq, d)
        h, xn = _moe(h, norm_ffn_w[i], router_w[i], router_b[i], w_gate_up, b_gate_up, w_down, b_down, i,
                     norm_mix_w[i + 1], False)
```

```python
import functools
import math

import jax
import jax.numpy as jnp
from jax import lax
from jax.experimental import pallas as pl
from jax.experimental.pallas import tpu as pltpu

F32 = jnp.float32
BF16 = jnp.bfloat16
HIGHEST = lax.Precision.HIGHEST

RMS_EPS = 1e-6
LOG2_E = 1.0 / math.log(2.0)
LANES = 128
SUBLANES = 8
V7X_VMEM_LIMIT_BYTES = 58 * 1024 * 1024

SSD_HEAD_DIM = 64
SSD_GROUPS = 8
SSD_HEADS_PER_GROUP = 8
SSD_STATE = 128
SSD_CONV = 4
SSD_CHUNK = 256
SSD_GROUP_W = SSD_HEADS_PER_GROUP * SSD_HEAD_DIM
SSD_PAIRS = SSD_GROUP_W // LANES

S5_GROUP_CH = 16
S5_STATE = 64
S5_CHUNK = 16
S5_OCT = LANES // S5_GROUP_CH

TOP_K = 4
SWIGLU_LIMIT = 7.0
SWIGLU_ALPHA = 1.702
MOE_TM = 1280
MOE_SUB = 256
MOE_TF = 256


def _params(semantics):
    return pltpu.CompilerParams(dimension_semantics=semantics, vmem_limit_bytes=V7X_VMEM_LIMIT_BYTES)


def _sigmoid(x):
    return 0.5 + 0.5 * jnp.tanh(0.5 * x)


def _silu(x):
    hx = 0.5 * x
    return hx + hx * jnp.tanh(hx)


def _pack_bf16_pair(lo, hi):
    ulo = lax.bitcast_convert_type(lo.astype(BF16).astype(F32), jnp.uint32)
    uhi = lax.bitcast_convert_type(hi.astype(BF16).astype(F32), jnp.uint32)
    return (ulo >> 16) | uhi


def _unpack_bf16_pair(w):
    lo = lax.bitcast_convert_type(w << 16, F32)
    hi = lax.bitcast_convert_type(w & jnp.uint32(0xFFFF0000), F32)
    return lo, hi


def _rms(x, w):
    ms = jnp.mean(x * x, axis=-1, keepdims=True)
    return x * lax.rsqrt(ms + RMS_EPS) * w


def _rmsnorm_body(x_ref, w_ref, o_ref):
    o_ref[...] = _rms(x_ref[...], w_ref[...]).astype(o_ref.dtype)


def _rmsnorm(x, w, out_dtype, tm=512):
    t, d = x.shape
    tm = min(tm, t)
    return pl.pallas_call(
        _rmsnorm_body,
        grid=(t // tm,),
        in_specs=[pl.BlockSpec((tm, d), lambda i: (i, 0)), pl.BlockSpec((1, d), lambda i: (0, 0))],
        out_specs=pl.BlockSpec((tm, d), lambda i: (i, 0)),
        out_shape=jax.ShapeDtypeStruct((t, d), out_dtype),
        compiler_params=_params(("parallel",)),
        name="rmsnorm",
    )(x, w.reshape(1, d))


def _mm_body(x_ref, w_ref, o_ref):
    o_ref[...] = jnp.dot(x_ref[...], w_ref[...], preferred_element_type=F32).astype(o_ref.dtype)


def _mm_res_body(x_ref, w_ref, r_ref, o_ref):
    o_ref[...] = r_ref[...] + jnp.dot(x_ref[...], w_ref[...], preferred_element_type=F32)


def _mm_glu_body(x_ref, wv_ref, wg_ref, r_ref, o_ref):
    x = x_ref[...].astype(BF16)
    v = jnp.dot(x, wv_ref[...], preferred_element_type=F32)
    g = jnp.dot(x, wg_ref[...], preferred_element_type=F32)
    o_ref[...] = r_ref[...] + v * _sigmoid(g)


def _matmul(x, w, out_dtype, tm, tn, name):
    m, k = x.shape
    n = w.shape[1]
    tm, tn = min(tm, m), min(tn, n)
    return pl.pallas_call(
        _mm_body,
        grid=(m // tm, n // tn),
        in_specs=[pl.BlockSpec((tm, k), lambda i, j: (i, 0)), pl.BlockSpec((k, tn), lambda i, j: (0, j))],
        out_specs=pl.BlockSpec((tm, tn), lambda i, j: (i, j)),
        out_shape=jax.ShapeDtypeStruct((m, n), out_dtype),
        compiler_params=_params(("parallel", "parallel")),
        name=name,
    )(x, w)


def _matmul_residual(x, w, res, tm, tn, name):
    m, k = x.shape
    n = w.shape[1]
    tm, tn = min(tm, m), min(tn, n)
    return pl.pallas_call(
        _mm_res_body,
        grid=(n // tn, m // tm),
        in_specs=[pl.BlockSpec((tm, k), lambda j, i: (i, 0)), pl.BlockSpec((k, tn), lambda j, i: (0, j)),
                  pl.BlockSpec((tm, tn), lambda j, i: (i, j))],
        out_specs=pl.BlockSpec((tm, tn), lambda j, i: (i, j)),
        out_shape=jax.ShapeDtypeStruct((m, n), F32),
        compiler_params=_params(("parallel", "parallel")),
        name=name,
    )(x, w, res)


def _matmul_glu_residual(x, w, res, tm, tn, name):
    m, k = x.shape
    n = w.shape[1] // 2
    tm, tn = min(tm, m), min(tn, n)
    nj = n // tn
    return pl.pallas_call(
        _mm_glu_body,
        grid=(nj, m // tm),
        in_specs=[pl.BlockSpec((tm, k), lambda j, i: (i, 0)),
                  pl.BlockSpec((k, tn), lambda j, i: (0, j)),
                  pl.BlockSpec((k, tn), lambda j, i: (0, nj + j)),
                  pl.BlockSpec((tm, tn), lambda j, i: (i, j))],
        out_specs=pl.BlockSpec((tm, tn), lambda j, i: (i, j)),
        out_shape=jax.ShapeDtypeStruct((m, n), F32),
        compiler_params=_params(("parallel", "parallel")),
        name=name,
    )(x, w, w, res)


def _ssd_body(z_ref, x_ref, b_ref, c_ref, dt_ref,
              cwx_ref, cwb_ref, cwc_ref, cbx_ref, cbb_ref, cbc_ref,
              dtb_ref, alog_ref, dexp_ref, nw_ref,
              o_ref,
              tx_ref, tb_ref, tc_ref, st_ref, pad_ref):
    c = pl.program_id(1)
    g = pl.program_id(2)
    lc = x_ref.shape[1]

    @pl.when(c == 0)
    def _():
        tx_ref[g] = jnp.zeros(tx_ref.shape[1:], F32)
        tb_ref[g] = jnp.zeros(tb_ref.shape[1:], F32)
        tc_ref[g] = jnp.zeros(tc_ref.shape[1:], F32)
        st_ref[g] = jnp.zeros(st_ref.shape[1:], F32)

    def conv_silu(u_ref, w_ref, bias_ref, tail_ref):
        width = u_ref.shape[2]
        u = u_ref[0].astype(F32)
        pad_ref[0:SUBLANES, 0:width] = tail_ref[g]
        pad_ref[SUBLANES:2 * SUBLANES, 0:width] = u[0:SUBLANES, :]
        tail_ref[g] = u[lc - SUBLANES:lc, :]
        acc = bias_ref[...] + w_ref[SSD_CONV - 1:SSD_CONV, :] * u
        for k in range(SSD_CONV - 1):
            sh = SSD_CONV - 1 - k
            body = pltpu.roll(u, sh, 0)
            head = pad_ref[SUBLANES - sh:2 * SUBLANES - sh, 0:width]
            acc = acc + w_ref[k:k + 1, :] * jnp.concatenate([head, body[SUBLANES:, :]], axis=0)
        return _silu(acc)

    xg = conv_silu(x_ref, cwx_ref, cbx_ref, tx_ref)
    bg = conv_silu(b_ref, cwb_ref, cbb_ref, tb_ref)
    cg = conv_silu(c_ref, cwc_ref, cbc_ref, tc_ref)

    dt_in = dt_ref[0] + dtb_ref[0]
    dt = jnp.maximum(dt_in, 0.0) + jnp.log1p(jnp.exp(-jnp.abs(dt_in)))
    a = dt * (-jnp.exp(alog_ref[0]))
    rows = lax.broadcasted_iota(jnp.int32, (lc, lc), 0)
    cols = lax.broadcasted_iota(jnp.int32, (lc, lc), 1)
    a_cum = jnp.dot((rows >= cols).astype(F32), a, precision=HIGHEST, preferred_element_type=F32)
    a2 = a_cum * LOG2_E
    a2_t = a2.T
    src_t = a2_t - jnp.log(dt.T) * LOG2_E

    bgt = bg.T
    cb = jnp.dot(cg.astype(BF16), bgt.astype(BF16), preferred_element_type=F32)
    left = lax.broadcasted_iota(jnp.int32, (1, LANES), 1) < SSD_HEAD_DIM
    n_blk = lc // LANES
    diag_mask = (lax.broadcasted_iota(jnp.int32, (LANES, LANES), 0) >= lax.broadcasted_iota(jnp.int32, (LANES, LANES), 1))
    neg_inf = jnp.float32(-jnp.inf)

    ys = []
    for p in range(SSD_PAIRS):
        xp = xg[:, p * LANES:(p + 1) * LANES]
        xpb = xp.astype(BF16)
        st = st_ref[g, p]
        stb = st.astype(BF16)
        y_h, st_h = [], []
        for hh in range(2):
            r = 2 * p + hh
            colb = jnp.broadcast_to(a2[:, r:r + 1], (lc, LANES))
            srow = src_t[r:r + 1, :]
            a2_tot = a2_t[r:r + 1, lc - 1:lc]
            cs = (cg * jnp.exp2(colb)).astype(BF16)
            y_rows = []
            for i in range(n_blk):
                ci = colb[i * LANES:(i + 1) * LANES]
                parts = []
                for j in range(i + 1):
                    seg = ci - srow[:, j * LANES:(j + 1) * LANES]
                    if j == i:
                        seg = jnp.where(diag_mask, seg, neg_inf)
                    parts.append((cb[i * LANES:(i + 1) * LANES, j * LANES:(j + 1) * LANES] * jnp.exp2(seg)).astype(BF16))
                lhs = jnp.concatenate(parts + [cs[i * LANES:(i + 1) * LANES]], axis=1)
                rhs = jnp.concatenate([xpb[:(i + 1) * LANES], stb], axis=0)
                y_rows.append(jnp.dot(lhs, rhs, preferred_element_type=F32))
            y_h.append(jnp.concatenate(y_rows, axis=0))
            w_row = jnp.exp2(a2_tot - srow)
            s_new = jnp.dot((bgt * w_row).astype(BF16), xpb, preferred_element_type=F32)
            st_h.append(st * jnp.exp2(a2_tot) + s_new)
        ys.append(jnp.where(left, y_h[0], y_h[1]) + dexp_ref[:, p * LANES:(p + 1) * LANES] * xp)
        st_ref[g, p] = jnp.where(left, st_h[0], st_h[1])

    y = jnp.concatenate(ys, axis=1)
    yz = y * _silu(z_ref[0].astype(F32))
    ms = jnp.mean(yz * yz, axis=-1, keepdims=True)
    o_ref[0] = (yz * lax.rsqrt(ms + RMS_EPS) * nw_ref[...]).astype(o_ref.dtype)


def _ssd(zx, dt, conv_w, conv_b, dt_bias, a_log, d_skip, norm_w):
    bsz, seq, _ = zx.shape
    g_n, gw, n = SSD_GROUPS, SSD_GROUP_W, SSD_STATE
    d_inner = g_n * gw
    lc = min(SSD_CHUNK, seq)
    nc = seq // lc
    xb0 = d_inner // gw
    bb0 = 2 * d_inner // n
    cb0 = bb0 + g_n
    cwx, cwb, cwc = conv_w[:, :d_inner], conv_w[:, d_inner:d_inner + g_n * n], conv_w[:, d_inner + g_n * n:]
    cb_ = conv_b.reshape(1, -1)
    cbx, cbb, cbc = cb_[:, :d_inner], cb_[:, d_inner:d_inner + g_n * n], cb_[:, d_inner + g_n * n:]
    pad_heads = lambda v: jnp.pad(v.reshape(g_n, 1, SSD_HEADS_PER_GROUP), ((0, 0), (0, 0), (0, LANES - SSD_HEADS_PER_GROUP)))
    dexp = jnp.repeat(d_skip, SSD_HEAD_DIM).reshape(1, d_inner)
    grp = lambda w: pl.BlockSpec((1, lc, w), lambda b, c, g: (b, c, g))
    return pl.pallas_call(
        _ssd_body,
        grid=(bsz, nc, g_n),
        in_specs=[
            pl.BlockSpec((1, lc, gw), lambda b, c, g: (b, c, g)),
            pl.BlockSpec((1, lc, gw), lambda b, c, g: (b, c, xb0 + g)),
            pl.BlockSpec((1, lc, n), lambda b, c, g: (b, c, bb0 + g)),
            pl.BlockSpec((1, lc, n), lambda b, c, g: (b, c, cb0 + g)),
            pl.BlockSpec((1, lc, LANES), lambda b, c, g: (b, c, g)),
            pl.BlockSpec((SSD_CONV, gw), lambda b, c, g: (0, g)),
            pl.BlockSpec((SSD_CONV, n), lambda b, c, g: (0, g)),
            pl.BlockSpec((SSD_CONV, n), lambda b, c, g: (0, g)),
            pl.BlockSpec((1, gw), lambda b, c, g: (0, g)),
            pl.BlockSpec((1, n), lambda b, c, g: (0, g)),
            pl.BlockSpec((1, n), lambda b, c, g: (0, g)),
            pl.BlockSpec((1, 1, LANES), lambda b, c, g: (g, 0, 0)),
            pl.BlockSpec((1, 1, LANES), lambda b, c, g: (g, 0, 0)),
            pl.BlockSpec((1, gw), lambda b, c, g: (0, g)),
            pl.BlockSpec((1, gw), lambda b, c, g: (0, g)),
        ],
        out_specs=pl.BlockSpec((1, lc, gw), lambda b, c, g: (b, c, g)),
        out_shape=jax.ShapeDtypeStruct((bsz, seq, d_inner), BF16),
        scratch_shapes=[
            pltpu.VMEM((g_n, SUBLANES, gw), F32),
            pltpu.VMEM((g_n, SUBLANES, n), F32),
            pltpu.VMEM((g_n, SUBLANES, n), F32),
            pltpu.VMEM((g_n, SSD_PAIRS, n, LANES), F32),
            pltpu.VMEM((2 * SUBLANES, gw), F32),
        ],
        compiler_params=_params(("arbitrary", "arbitrary", "arbitrary")),
        name="ssd_scan",
    )(zx, zx, zx, zx, dt, cwx, cwb, cwc, cbx, cbb, cbc, pad_heads(dt_bias), pad_heads(a_log), dexp,
      norm_w.reshape(1, d_inner))


def _ssd_mixer(h, xn, in_proj, conv_w, conv_b, dt_bias, a_log, d_skip, norm_w, out_proj, bsz, seq):
    t, d = h.shape
    n_heads = dt_bias.shape[0]
    d_main = in_proj.shape[1] - n_heads
    w_main = in_proj[:, :d_main].astype(BF16)
    w_dt = jnp.pad(in_proj[:, d_main:], ((0, 0), (0, LANES - n_heads))).astype(BF16)
    zx = _matmul(xn, w_main, BF16, 1024, 1024, "ssd_in_proj")
    dt_raw = _matmul(xn, w_dt, F32, 1024, LANES, "ssd_dt_proj")
    dt = dt_raw[:, :n_heads].reshape(t, SSD_GROUPS, SSD_HEADS_PER_GROUP)
    dt = jnp.pad(dt, ((0, 0), (0, 0), (0, LANES - SSD_HEADS_PER_GROUP))).reshape(bsz, seq, SSD_GROUPS * LANES)
    y = _ssd(zx.reshape(bsz, seq, d_main), dt, conv_w, conv_b, dt_bias, a_log, d_skip, norm_w)
    return _matmul_residual(y.reshape(t, -1), out_proj.astype(BF16), h, 512, 1024, "ssd_out_proj")


def _s5_body(u_ref, d_ref, win_ref, wout_ref, are_ref, aim_ref, o_ref, *, n_chunks):
    r = u_ref.shape[0] // S5_CHUNK
    pw = 2 * LANES
    n_pairs = S5_CHUNK // 2
    n_state = are_ref.shape[2]
    z = jnp.concatenate([u_ref[pl.ds(s, r, stride=S5_CHUNK), :] for s in range(S5_CHUNK)], axis=1).astype(BF16)
    xin = jnp.dot(z, win_ref[0], preferred_element_type=F32)
    xre, xim = xin[:, :n_state], xin[:, n_state:]
    kidx = lax.broadcasted_iota(jnp.int32, (r, n_state), 0) % n_chunks

    def shifted(v, sh):
        return jnp.where(kidx >= sh, pltpu.roll(v, sh, 0), 0.0)

    sh, lvl = 1, 0
    while sh < n_chunks:
        are, aim = are_ref[0, lvl:lvl + 1, :], aim_ref[0, lvl:lvl + 1, :]
        sre, sim = shifted(xre, sh), shifted(xim, sh)
        xre, xim = xre + are * sre - aim * sim, xim + are * sim + aim * sre
        sh, lvl = sh * 2, lvl + 1
    prev = jnp.concatenate([shifted(xre, 1), shifted(xim, 1)], axis=1).astype(BF16)
    carry = jnp.dot(prev, wout_ref[0], preferred_element_type=F32)
    for tp in range(n_pairs):
        y2 = jnp.dot(z[:, :(tp + 1) * pw], d_ref[0, (n_pairs - 1 - tp) * pw:, :], preferred_element_type=F32)
        y2 = y2 + carry[:, tp * pw:(tp + 1) * pw]
        gelu = 0.5 * y2 * (1.0 + jnp.tanh(math.sqrt(2.0 / math.pi) * (y2 + 0.044715 * (y2 * y2 * y2))))
        for i in range(2):
            o_ref[pl.ds(2 * tp + i, r, stride=S5_CHUNK), :] = gelu[:, i * LANES:(i + 1) * LANES]


def _s5_tables(lam_re, lam_im, log_step, b_re, b_im, c_re, c_im, d_skip, n_chunks):
    g_n, p_n = lam_re.shape
    ch, lcs, oct_ = S5_GROUP_CH, S5_CHUNK, S5_OCT
    n_oct = g_n // oct_
    lam = lax.complex(lam_re, lam_im)
    step = jnp.exp(log_step)[:, None]
    lam_bar = jnp.exp(lam * step)
    b_bar = ((lam_bar - 1.0) / lam)[..., None] * lax.complex(b_re, b_im)
    cc = lax.complex(c_re, c_im)
    taus = jnp.arange(lcs + 1, dtype=F32)
    lam_pow = jnp.exp((lam * step)[None] * taus[:, None, None])

    def spread(w, src_lane, grp_lane, g_axis):
        sel = (jnp.arange(w.shape[-1])[:, None] == src_lane[None, :]).astype(BF16)
        v = jnp.dot(w.astype(BF16), sel, preferred_element_type=F32)
        gshape = [1] * v.ndim
        gshape[g_axis] = oct_
        return jnp.where(jnp.arange(oct_).reshape(gshape) == grp_lane, v, 0.0).astype(BF16)

    kern = jnp.einsum('gcp,tgp,gpd->gtcd', cc, lam_pow[:lcs], b_bar, precision=HIGHEST).real
    kern = kern.at[:, 0].add(d_skip.reshape(g_n, ch)[:, :, None] * jnp.eye(ch, dtype=F32))
    kz = jnp.concatenate([jnp.zeros((g_n, 1, ch, ch), F32), kern], axis=1)
    dl = jnp.arange(lcs // 2)[:, None, None]
    s2 = jnp.arange(2)[None, :, None]
    t2 = jnp.arange(2)[None, None, :]
    blk = kz[:, 2 * dl + t2 - s2 + 1]
    blk = jnp.transpose(blk.reshape(n_oct, oct_, lcs // 2, 2, 2, ch, ch), (0, 2, 3, 1, 6, 4, 5))
    lane2 = jnp.arange(2 * LANES)
    dtab = spread(blk.reshape(n_oct, lcs // 2, 2, oct_, ch, 2 * ch),
                  (lane2 // LANES) * ch + lane2 % ch, (lane2 % LANES) // ch, 3)
    dtab = dtab.reshape(n_oct, lcs // 2, 2 * LANES, 2 * LANES)[:, ::-1].reshape(n_oct, lcs * LANES, 2 * LANES)
    win = jnp.einsum('sgp,gpd->gsdp', lam_pow[:lcs][::-1], b_bar)
    win = jnp.transpose(win.reshape(n_oct, oct_, lcs, ch, p_n), (0, 2, 1, 3, 4))
    n_state = oct_ * p_n
    lane_s = jnp.arange(2 * n_state)
    win_t = spread(jnp.concatenate([win.real, win.imag], axis=-1),
                   (lane_s // n_state) * p_n + lane_s % p_n, (lane_s % n_state) // p_n, 2)
    win_t = win_t.reshape(n_oct, lcs * LANES, 2 * n_state)
    wout = jnp.einsum('gcp,tgp->gptc', cc, lam_pow[1:lcs + 1]).reshape(n_oct, oct_, p_n, lcs * ch)
    lane_o = jnp.arange(lcs * LANES)
    src_o, grp_o = (lane_o // LANES) * ch + lane_o % ch, (lane_o % LANES) // ch
    wout_t = jnp.concatenate([spread(wout.real, src_o, grp_o, 1).reshape(n_oct, n_state, lcs * LANES),
                              spread(-wout.imag, src_o, grp_o, 1).reshape(n_oct, n_state, lcs * LANES)], axis=1)
    n_lvl = max(1, int(math.ceil(math.log2(max(n_chunks, 2)))))
    lvl_pow = lcs * (2.0 ** jnp.arange(n_lvl, dtype=F32))
    a_pow = jnp.exp((lam * step)[None] * lvl_pow[:, None, None])
    a_pow = jnp.transpose(a_pow.reshape(n_lvl, n_oct, oct_ * p_n), (1, 0, 2))
    a_pad = ((0, 0), (0, -n_lvl % SUBLANES), (0, 0))
    return (dtab, win_t, wout_t,
            jnp.pad(a_pow.real, a_pad), jnp.pad(a_pow.imag, a_pad))


def _s5_mixer(h, xn, in_proj, lam_re, lam_im, log_step, b_re, b_im, c_re, c_im, d_skip, glu_proj, bsz, seq):
    t, d = h.shape
    g_n, p_n = lam_re.shape
    width = g_n * S5_GROUP_CH
    n_chunks = seq // S5_CHUNK
    n_oct = g_n // S5_OCT
    u = _matmul(xn, in_proj.astype(BF16), F32, 1024, 1024, "s5_in_proj")
    dtab, win_t, wout_t, are, aim = _s5_tables(lam_re, lam_im, log_step, b_re, b_im, c_re, c_im, d_skip, n_chunks)
    n_split = 2 if bsz % 2 == 0 else 1
    rows = t // n_split
    tab = lambda a: pl.BlockSpec((1,) + a.shape[1:], lambda j, i: (j, 0, 0))
    gact = pl.pallas_call(
        functools.partial(_s5_body, n_chunks=n_chunks),
        grid=(n_oct, n_split),
        in_specs=[pl.BlockSpec((rows, LANES), lambda j, i: (i, j)), tab(dtab), tab(win_t), tab(wout_t), tab(are), tab(aim)],
        out_specs=pl.BlockSpec((rows, LANES), lambda j, i: (i, j)),
        out_shape=jax.ShapeDtypeStruct((t, width), F32),
        compiler_params=_params(("parallel", "parallel")),
        name="s5_scan",
    )(u, dtab, win_t, wout_t, are, aim)
    return _matmul_glu_residual(gact, glu_proj.astype(BF16), h, 512, 1024, "s5_glu_proj")


def _router_body(h_ref, nw_ref, rw_ref, rb_ref, idx_ref, gate_ref, *, n_experts):
    xn = _rms(h_ref[...], nw_ref[...])
    xh = xn.astype(BF16)
    xm = (xn - xh.astype(F32)).astype(BF16)
    logits = jnp.dot(jnp.concatenate([xh, xh, xm], axis=1), rw_ref[...], preferred_element_type=F32) + rb_ref[...]
    tm = logits.shape[0]
    lane = lax.broadcasted_iota(jnp.int32, (tm, LANES), 1)
    lane_f = lane.astype(F32)
    neg_inf = jnp.float32(-jnp.inf)
    cur = jnp.where(lane < n_experts, logits, neg_inf)
    tops, idxs = [], []
    for _ in range(TOP_K):
        mx = jnp.max(cur, axis=-1, keepdims=True)
        ix = jnp.min(jnp.where(cur == mx, lane_f, float(LANES)), axis=-1, keepdims=True)
        cur = jnp.where(lane_f == ix, neg_inf, cur)
        tops.append(mx)
        idxs.append(ix)
    es = [jnp.exp(v - tops[0]) for v in tops]
    denom = es[0] + es[1] + es[2] + es[3]
    idx_out = jnp.zeros((tm, LANES), F32)
    gate_out = jnp.zeros((tm, LANES), F32)
    for k in range(TOP_K):
        idx_out = jnp.where(lane == k, idxs[k], idx_out)
        gate_out = jnp.where(lane == k, es[k] / denom, gate_out)
    idx_ref[...] = idx_out.astype(jnp.int32)
    gate_ref[...] = gate_out


def _rank_body(idx_ref, rank_ref, cnt_ref, carry_ref):
    i = pl.program_id(0)
    tb = idx_ref.shape[0]

    @pl.when(i == 0)
    def _():
        carry_ref[...] = jnp.zeros(carry_ref.shape, F32)

    idx = idx_ref[...]
    lane = lax.broadcasted_iota(jnp.int32, (tb, LANES), 1)
    onehots = [(lane == idx[:, k:k + 1]) for k in range(TOP_K)]
    hits = jnp.zeros((tb, LANES), F32)
    for oh in onehots:
        hits = hits + oh.astype(F32)
    rows = lax.broadcasted_iota(jnp.int32, (tb, tb), 0)
    cols = lax.broadcasted_iota(jnp.int32, (tb, tb), 1)
    strict = (rows > cols).astype(BF16)
    before = jnp.dot(strict, hits.astype(BF16), preferred_element_type=F32) + carry_ref[0:1, :]
    out = jnp.zeros((tb, LANES), jnp.int32)
    for k, oh in enumerate(onehots):
        rk = jnp.sum(jnp.where(oh, before, 0.0), axis=-1, keepdims=True)
        out = jnp.where(lane == k, rk.astype(jnp.int32), out)
    rank_ref[...] = out
    total = carry_ref[0:1, :] + jnp.sum(hits, axis=0, keepdims=True)
    carry_ref[...] = jnp.broadcast_to(total, carry_ref.shape)
    cnt_ref[...] = jnp.broadcast_to(total, cnt_ref.shape)


def _dispatch_body(dest_ref, rib_ref, h_ref, nw_ref, xs_ref, buf_ref, zero_ref, sems, zero_sem, *, n_steps):
    tt, d = h_ref.shape
    step = pl.program_id(0)
    slot = lax.rem(step, 2)

    @pl.when(step == 0)
    def _():
        zero_ref[...] = jnp.zeros(zero_ref.shape, zero_ref.dtype)

        def zero_copy(b, j):
            return pltpu.make_async_copy(zero_ref, xs_ref.at[pl.ds(b * MOE_TM + j * MOE_SUB, MOE_SUB), :], zero_sem)

        def zero_block(op):
            def body(b, carry):
                rows_in_block = rib_ref[b]
                for j in range(MOE_TM // MOE_SUB):
                    @pl.when(rows_in_block < (j + 1) * MOE_SUB)
                    def _():
                        op(zero_copy(b, j))
                return carry
            return body

        lax.fori_loop(0, rib_ref.shape[0], zero_block(lambda cp: cp.start()), 0)
        lax.fori_loop(0, rib_ref.shape[0], zero_block(lambda cp: cp.wait()), 0)

    def drain(s):
        for _ in range(TOP_K):
            pltpu.make_async_copy(buf_ref.at[s], xs_ref.at[pl.ds(0, tt), :], sems.at[s]).wait()

    @pl.when(step >= 2)
    def _():
        drain(slot)

    xn = _rms(h_ref[...], nw_ref[...])
    buf_ref[slot] = _pack_bf16_pair(xn[:, :d // 2], xn[:, d // 2:])

    for s in range(2):
        @pl.when(slot == s)
        def _():
            def start(i, carry):
                for k in range(TOP_K):
                    row = dest_ref[0, 0, i * TOP_K + k]
                    pltpu.make_async_copy(buf_ref.at[s, pl.ds(i, 1), :], xs_ref.at[pl.ds(row, 1), :], sems.at[s]).start()
                return carry

            lax.fori_loop(0, tt, start, 0, unroll=2)

    @pl.when(step == n_steps - 1)
    def _():
        if n_steps >= 2:
            drain(1 - slot)
        drain(slot)


def _combine_body(dest_ref, dest_next_ref, h_ref, gate_ref, nw_ref, ys_ref, *rest, final, n_steps):
    if final:
        o_ref, buf_ref, sems = rest
    else:
        o_ref, on_ref, buf_ref, sems = rest
    tt = h_ref.shape[0]
    step = pl.program_id(0)
    slot = lax.rem(step, 2)

    def issue(idx_ref, s):
        def start(i, carry):
            for k in range(TOP_K):
                row = idx_ref[0, 0, i * TOP_K + k]
                pltpu.make_async_copy(ys_ref.at[pl.ds(row, 1), :], buf_ref.at[s, pl.ds(k * tt + i, 1), :],
                                      sems.at[s]).start()
            return carry
        lax.fori_loop(0, tt, start, 0, unroll=2)

    @pl.when(step == 0)
    def _():
        issue(dest_ref, 0)

    for s in range(2):
        @pl.when((step + 1 < n_steps) & (slot == s))
        def _():
            issue(dest_next_ref, 1 - s)

    pltpu.make_async_copy(ys_ref.at[pl.ds(0, TOP_K * tt), :], buf_ref.at[slot], sems.at[slot]).wait()
    gates = gate_ref[...]
    h = h_ref[...]
    half = h.shape[1] // 2
    acc_lo, acc_hi = h[:, :half], h[:, half:]
    for k in range(TOP_K):
        lo, hi = _unpack_bf16_pair(buf_ref[slot, k * tt:(k + 1) * tt, :])
        acc_lo = acc_lo + gates[:, k:k + 1] * lo
        acc_hi = acc_hi + gates[:, k:k + 1] * hi
    acc = jnp.concatenate([acc_lo, acc_hi], axis=1)
    if final:
        o_ref[...] = _rms(acc, nw_ref[...])
    else:
        o_ref[...] = acc
        on_ref[...] = _rms(acc, nw_ref[...]).astype(on_ref.dtype)


def _expert_body(be_ref, ns_ref, nu_ref, x_ref, wg_ref, wu_ref, wd_ref, bg_ref, bu_ref, bd_ref, o_ref, acc_ref):
    del be_ref, nu_ref
    b = pl.program_id(0)
    f = pl.program_id(1)
    n_sub = ns_ref[b]
    d = acc_ref.shape[1]

    @pl.when(f == 0)
    def _():
        acc_ref[...] = jnp.broadcast_to(bd_ref[0, 0], acc_ref.shape)

    for ns in range(1, MOE_TM // MOE_SUB + 1):
        @pl.when(n_sub == ns)
        def _():
            rows = ns * MOE_SUB
            lo, hi = _unpack_bf16_pair(x_ref[0:rows, :])
            x = jnp.concatenate([lo.astype(BF16), hi.astype(BF16)], axis=1)
            gate = jnp.dot(x, wg_ref[0, 0].astype(BF16), preferred_element_type=F32) + bg_ref[0, 0]
            up = jnp.dot(x, wu_ref[0, 0].astype(BF16), preferred_element_type=F32) + bu_ref[0, 0]
            gate = jnp.minimum(gate, SWIGLU_LIMIT)
            up = jnp.clip(up, -SWIGLU_LIMIT, SWIGLU_LIMIT)
            act = ((up + 1.0) * (gate * _sigmoid(SWIGLU_ALPHA * gate))).astype(BF16)
            for half in range(2):
                cs = slice(half * d // 2, (half + 1) * d // 2)
                wd = wd_ref[0, 0, :, cs].astype(BF16)
                acc_ref[0:rows, cs] = acc_ref[0:rows, cs] + jnp.dot(act, wd, preferred_element_type=F32)

    @pl.when(f == pl.num_programs(1) - 1)
    def _():
        o_ref[...] = _pack_bf16_pair(acc_ref[:, :d // 2], acc_ref[:, d // 2:])


def _moe(h, norm_w, router_w, router_b, w_gate_up, b_gate_up, w_down, b_down, layer, next_norm_w, final):
    t, d = h.shape
    n_e = router_w.shape[1]
    d_ff = w_down.shape[2]
    nw = norm_w.reshape(1, d)
    tr = min(512, t)
    rw = jnp.pad(router_w, ((0, 0), (0, LANES - n_e)))
    rw_hi = rw.astype(BF16)
    rw_mid = (rw - rw_hi.astype(F32)).astype(BF16)
    rw = jnp.concatenate([rw_hi, rw_mid, rw_hi], axis=0)
    rb = jnp.pad(router_b, (0, LANES - n_e)).reshape(1, LANES)
    idx, gates = pl.pallas_call(
        functools.partial(_router_body, n_experts=n_e),
        grid=(t // tr,),
        in_specs=[pl.BlockSpec((tr, d), lambda i: (i, 0)), pl.BlockSpec((1, d), lambda i: (0, 0)),
                  pl.BlockSpec((3 * d, LANES), lambda i: (0, 0)), pl.BlockSpec((1, LANES), lambda i: (0, 0))],
        out_specs=[pl.BlockSpec((tr, LANES), lambda i: (i, 0)), pl.BlockSpec((tr, LANES), lambda i: (i, 0))],
        out_shape=[jax.ShapeDtypeStruct((t, LANES), jnp.int32), jax.ShapeDtypeStruct((t, LANES), F32)],
        compiler_params=_params(("parallel",)),
        name="moe_router",
    )(h, nw, rw, rb)
    tb = min(512, t)
    rank, cnt = pl.pallas_call(
        _rank_body,
        grid=(t // tb,),
        in_specs=[pl.BlockSpec((tb, LANES), lambda i: (i, 0))],
        out_specs=[pl.BlockSpec((tb, LANES), lambda i: (i, 0)), pl.BlockSpec((SUBLANES, LANES), lambda i: (0, 0))],
        out_shape=[jax.ShapeDtypeStruct((t, LANES), jnp.int32), jax.ShapeDtypeStruct((SUBLANES, LANES), F32)],
        scratch_shapes=[pltpu.VMEM((SUBLANES, LANES), F32)],
        compiler_params=_params(("arbitrary",)),
        name="moe_rank",
    )(idx)
    counts = cnt[0, :n_e].astype(jnp.int32)
    n_blocks = -(-t * TOP_K // MOE_TM) + n_e
    blocks_e = (counts + MOE_TM - 1) // MOE_TM
    blk_end = jnp.cumsum(blocks_e)
    blk_start = blk_end - blocks_e
    n_used = blk_end[-1]
    bid = jnp.arange(n_blocks, dtype=jnp.int32)
    blk_expert = jnp.minimum(jnp.searchsorted(blk_end, bid, side='right'), n_e - 1).astype(jnp.int32)
    last_used_expert = blk_expert[jnp.maximum(n_used - 1, 0)]
    blk_expert = jnp.where(bid < n_used, blk_expert, last_used_expert)
    rows_in_blk = jnp.clip(counts[blk_expert] - (bid - blk_start[blk_expert]) * MOE_TM, 0, MOE_TM)
    n_sub = jnp.where(bid < n_used, (rows_in_blk + MOE_SUB - 1) // MOE_SUB, 0).astype(jnp.int32)
    dest = blk_start[idx[:, :TOP_K]] * MOE_TM + rank[:, :TOP_K]
    tt = min(256, t)
    dest3 = dest.reshape(t // tt, 1, tt * TOP_K)
    n_rows = n_blocks * MOE_TM
    dp = d // 2
    xs = pl.pallas_call(
        functools.partial(_dispatch_body, n_steps=t // tt),
        grid=(t // tt,),
        in_specs=[pl.BlockSpec((1, 1, tt * TOP_K), lambda i: (i, 0, 0), memory_space=pltpu.SMEM),
                  pl.BlockSpec(memory_space=pltpu.SMEM),
                  pl.BlockSpec((tt, d), lambda i: (i, 0)), pl.BlockSpec((1, d), lambda i: (0, 0))],
        out_specs=pl.BlockSpec(memory_space=pl.ANY),
        out_shape=jax.ShapeDtypeStruct((n_rows, dp), jnp.uint32),
        scratch_shapes=[pltpu.VMEM((2, tt, dp), jnp.uint32), pltpu.VMEM((MOE_SUB, dp), jnp.uint32),
                        pltpu.SemaphoreType.DMA((2,)), pltpu.SemaphoreType.DMA],
        compiler_params=_params(("arbitrary",)),
        name="moe_dispatch",
    )(dest3, jnp.where(bid < n_used, rows_in_blk, 0).astype(jnp.int32), h, nw)
    nf = d_ff // MOE_TF
    last_f = nf - 1
    used = lambda b, f, nu: jnp.where(b < nu[0], f, last_f)
    ys = pl.pallas_call(
        _expert_body,
        grid_spec=pltpu.PrefetchScalarGridSpec(
            num_scalar_prefetch=3,
            grid=(n_blocks, nf),
            in_specs=[
                pl.BlockSpec((MOE_TM, dp), lambda b, f, be, ns, nu: (jnp.minimum(b, jnp.maximum(nu[0] - 1, 0)), 0)),
                pl.BlockSpec((1, 1, d, MOE_TF), lambda b, f, be, ns, nu: (layer, be[b], 0, used(b, f, nu))),
                pl.BlockSpec((1, 1, d, MOE_TF), lambda b, f, be, ns, nu: (layer, be[b], 0, nf + used(b, f, nu))),
                pl.BlockSpec((1, 1, MOE_TF, d), lambda b, f, be, ns, nu: (layer, be[b], used(b, f, nu), 0)),
                pl.BlockSpec((1, 1, 1, MOE_TF), lambda b, f, be, ns, nu: (layer, be[b], 0, used(b, f, nu))),
                pl.BlockSpec((1, 1, 1, MOE_TF), lambda b, f, be, ns, nu: (layer, be[b], 0, nf + used(b, f, nu))),
                pl.BlockSpec((1, 1, 1, d), lambda b, f, be, ns, nu: (layer, be[b], 0, 0)),
            ],
            out_specs=pl.BlockSpec((MOE_TM, dp), lambda b, f, be, ns, nu: (b, 0)),
            scratch_shapes=[pltpu.VMEM((MOE_TM, d), F32)],
        ),
        out_shape=jax.ShapeDtypeStruct((n_rows, dp), jnp.uint32),
        compiler_params=_params(("arbitrary", "arbitrary")),
        name="moe_experts",
    )(blk_expert, n_sub, n_used.reshape(1).astype(jnp.int32), xs, w_gate_up, w_gate_up, w_down,
      b_gate_up.reshape(-1, n_e, 1, 2 * d_ff), b_gate_up.reshape(-1, n_e, 1, 2 * d_ff), b_down.reshape(-1, n_e, 1, d))
    tc = min(256, t)
    destc = dest.reshape(t // tc, 1, tc * TOP_K)
    row_spec = pl.BlockSpec((tc, d), lambda i: (i, 0))
    out_specs, out_shape = row_spec, jax.ShapeDtypeStruct((t, d), F32)
    if not final:
        out_specs, out_shape = [row_spec, row_spec], [out_shape, jax.ShapeDtypeStruct((t, d), BF16)]
    n_comb = t // tc
    return pl.pallas_call(
        functools.partial(_combine_body, final=final, n_steps=n_comb),
        grid=(n_comb,),
        in_specs=[pl.BlockSpec((1, 1, tc * TOP_K), lambda i: (i, 0, 0), memory_space=pltpu.SMEM),
                  pl.BlockSpec((1, 1, tc * TOP_K), lambda i: (jnp.minimum(i + 1, n_comb - 1), 0, 0),
                               memory_space=pltpu.SMEM),
                  row_spec, pl.BlockSpec((tc, LANES), lambda i: (i, 0)), pl.BlockSpec((1, d), lambda i: (0, 0)),
                  pl.BlockSpec(memory_space=pl.ANY)],
        out_specs=out_specs,
        out_shape=out_shape,
        scratch_shapes=[pltpu.VMEM((2, TOP_K * tc, dp), jnp.uint32), pltpu.SemaphoreType.DMA((2,))],
        compiler_params=_params(("arbitrary",)),
        name="moe_combine",
    )(destc, destc, h, gates, next_norm_w.reshape(1, d), ys)


def kernel(x, norm_mix_w, norm_ffn_w, final_norm_w, a_in_proj, a_conv_w, a_conv_b, a_dt_bias, a_log, a_d_skip, a_norm_w, a_out_proj, b_in_proj, b_lam_re, b_lam_im, b_log_step, b_b_re, b_b_im, b_c_re, b_c_im, b_d_skip, b_glu_proj, router_w, router_b, w_gate_up, b_gate_up, w_down, b_down):
    bsz, seq, d = x.shape
    depth = norm_mix_w.shape[0]
    h = x.reshape(bsz * seq, d)
    xn = _rmsnorm(h, norm_mix_w[0], BF16)
    for i in range(depth):
        j = i // 2
        if i % 2 == 0:
            h = _ssd_mixer(h, xn, a_in_proj[j], a_conv_w[j], a_conv_b[j], a_dt_bias[j], a_log[j], a_d_skip[j],
                           a_norm_w[j], a_out_proj[j], bsz, seq)
        else:
            h = _s5_mixer(h, xn, b_in_proj[j], b_lam_re[j], b_lam_im[j], b_log_step[j], b_b_re[j], b_b_im[j],
                          b_c_re[j], b_c_im[j], b_d_skip[j], b_glu_proj[j], bsz, seq)
        if i == depth - 1:
            out = _moe(h, norm_ffn_w[i], router_w[i], router_b[i], w_gate_up, b_gate_up, w_down, b_down, i,
                       final_norm_w, True)
            return out.reshape(bsz, seq, d)
        h, xn = _moe(h, norm_ffn_w[i], router_w[i], router_b[i], w_gate_up, b_gate_up, w_down, b_down, i,
                     norm_mix_w[i + 1], False)
```
